```python
import math
import jax, jax.numpy as jnp
from jax import lax
import numpy as np

D_MODEL = 1024
BATCH = 16
SEQ = 2048
DEPTH = 4

N_MIXERS = 4
PLE_DIM = 256
ALPHA = (2 * DEPTH) ** 0.25
BETA_INIT = (8 * DEPTH) ** -0.25
LN_EPS = 1e-5

GM_CHUNK = 128
GM_WIDTH = 2 * D_MODEL
GM_GROUPS = 8
FOX_HEADS = 16
FOX_HEAD_DIM = D_MODEL // FOX_HEADS
FOX_BLOCK = 128
GDN_HEADS = 8
GDN_DK = 128
GDN_DV = 128
GDN_CONV = 4
GDN_CHUNK = 64
SSD_INNER = 2 * D_MODEL
SSD_HEAD_DIM = 64
SSD_HEADS = SSD_INNER // SSD_HEAD_DIM
SSD_GROUPS = 4
SSD_STATE = 128
SSD_CONV = 4
SSD_CHUNK = 64
D_FF = 3584
N_EXPERTS = 8
TOP_K = 2

N_LAYERS_PER_MIXER = tuple((DEPTH - m + N_MIXERS - 1) // N_MIXERS for m in range(N_MIXERS))
N_DENSE = (DEPTH + 1) // 2
N_MOE = DEPTH // 2

kernel_name = "hybrid_interleaved_gmlp_fox_gdn_ssd_moe"


def layer_norm(x, g, b):
    xf = x.astype(jnp.float32)
    mu = jnp.mean(xf, axis=-1, keepdims=True)
    var = jnp.mean(jnp.square(xf - mu), axis=-1, keepdims=True)
    return ((xf - mu) * lax.rsqrt(var + LN_EPS) * g.astype(jnp.float32) + b.astype(jnp.float32)).astype(x.dtype)


def rms_norm(x, g, eps=1e-6):
    xf = x.astype(jnp.float32)
    return (xf * lax.rsqrt(jnp.mean(xf * xf, axis=-1, keepdims=True) + eps) * g.astype(jnp.float32)).astype(x.dtype)


def l2_norm(x, eps=1e-6):
    return x * lax.rsqrt(jnp.sum(x * x, axis=-1, keepdims=True) + eps)


def causal_dwconv(x, w):
    k, c = w.shape
    return lax.conv_general_dilated(x, w[:, None, :].astype(x.dtype), window_strides=(1,), padding=[(k - 1, 0)],
                                    dimension_numbers=("NWC", "WIO", "NWC"), feature_group_count=c)


def gmlp_mixer(x, w_in, b_in, ln_g, ln_b, w_s, b_s, w_out):
    bsz, s, _ = x.shape
    z = jax.nn.gelu(x @ w_in + b_in)
    u, v = jnp.split(z, 2, axis=-1)
    v = layer_norm(v, ln_g, ln_b)
    v = v.reshape(bsz, s // GM_CHUNK, GM_CHUNK, GM_GROUPS, GM_WIDTH // GM_GROUPS)
    mask = jnp.tril(jnp.ones((GM_CHUNK, GM_CHUNK), dtype=bool))
    ws = jnp.where(mask, w_s, jnp.zeros_like(w_s))
    sv = jnp.einsum("gts,bnsgc->bntgc", ws, v) + b_s.T[None, None, :, :, None]
    return (u * sv.reshape(bsz, s, GM_WIDTH)) @ w_out


def fox_mixer(x, w_in, b_f, w_out):
    bsz, s, _ = x.shape
    h, dh = FOX_HEADS, FOX_HEAD_DIM
    width = h * dh
    q, k, v, f = jnp.split(x @ w_in, [width, 2 * width, 3 * width], axis=-1)
    q = q.reshape(bsz, s, h, dh).transpose(0, 2, 1, 3) * (dh ** -0.5)
    k = k.reshape(bsz, s, h, dh).transpose(0, 2, 1, 3)
    v = v.reshape(bsz, s, h, dh).transpose(0, 2, 1, 3)
    log_f = jax.nn.log_sigmoid((f + b_f).astype(jnp.float32))
    c = jnp.cumsum(log_f, axis=1).transpose(0, 2, 1)
    outs = []
    for blk in range(s // FOX_BLOCK):
        q0, q1 = blk * FOX_BLOCK, (blk + 1) * FOX_BLOCK
        logits = jnp.einsum("bhtd,bhsd->bhts", q[:, :, q0:q1], k[:, :, :q1]).astype(jnp.float32)
        logits = logits + c[:, :, q0:q1, None] - c[:, :, None, :q1]
        mask = (q0 + jnp.arange(FOX_BLOCK))[:, None] >= jnp.arange(q1)[None, :]
        logits = jnp.where(mask, logits, -jnp.inf)
        probs = jax.nn.softmax(logits, axis=-1).astype(v.dtype)
        outs.append(jnp.einsum("bhts,bhsd->bhtd", probs, v[:, :, :q1]))
    o = jnp.concatenate(outs, axis=2).transpose(0, 2, 1, 3).reshape(bsz, s, width)
    return o @ w_out


def gated_delta_chunked(q, k, v, beta, g):
    bsz, s, h, dk = q.shape
    dv = v.shape[-1]
    c = GDN_CHUNK
    n = s // c
    to_chunks = lambda t: t.reshape(bsz, n, c, h, -1).transpose(0, 3, 1, 2, 4)
    q, k, v = to_chunks(q), to_chunks(k), to_chunks(v)
    beta = beta.reshape(bsz, n, c, h).transpose(0, 3, 1, 2)
    gc = jnp.cumsum(g.reshape(bsz, n, c, h).transpose(0, 3, 1, 2), axis=-1)
    causal = jnp.tril(jnp.ones((c, c), dtype=bool))
    strict = jnp.tril(jnp.ones((c, c), dtype=bool), -1)
    decay = jnp.exp(jnp.where(causal, gc[..., :, None] - gc[..., None, :], -jnp.inf))
    kb = k * beta[..., None]
    m = jnp.where(strict, jnp.einsum("bhnid,bhnjd->bhnij", kb, k) * decay, 0.0)
    a = m + jnp.eye(c, dtype=m.dtype)
    u = lax.linalg.triangular_solve(a, v * beta[..., None], left_side=True, lower=True, unit_diagonal=True)
    w = lax.linalg.triangular_solve(a, kb * jnp.exp(gc)[..., None], left_side=True, lower=True, unit_diagonal=True)
    aqk = jnp.einsum("bhnid,bhnjd->bhnij", q, k) * decay

    def step(state, inp):
        qc, kc, uc, wc, ac, gcc = inp
        v_new = uc - jnp.einsum("bhcd,bhde->bhce", wc, state)
        o = jnp.einsum("bhcd,bhde->bhce", qc * jnp.exp(gcc)[..., None], state) + jnp.einsum("bhij,bhje->bhie", ac, v_new)
        g_last = gcc[..., -1]
        state = state * jnp.exp(g_last)[..., None, None] + jnp.einsum(
            "bhcd,bhce->bhde", kc * jnp.exp(g_last[..., None] - gcc)[..., None], v_new)
        return state, o

    xs = tuple(jnp.moveaxis(t, 2, 0) for t in (q, k, u, w, aqk, gc))
    state0 = jnp.zeros((bsz, h, dk, dv), jnp.float32)
    _, o = lax.scan(step, state0, xs)
    return o.transpose(1, 0, 3, 2, 4).reshape(bsz, s, h, dv)


def gdn_mixer(x, w_in, conv_w, a_log, dt_bias, norm_g, w_out):
    bsz, s, _ = x.shape
    h, dk, dv = GDN_HEADS, GDN_DK, GDN_DV
    qk_w, v_w = h * dk, h * dv
    qkv, gate, b_logit, a_logit = jnp.split(x @ w_in, [2 * qk_w + v_w, 2 * qk_w + 2 * v_w, 2 * qk_w + 2 * v_w + h], axis=-1)
    qkv = jax.nn.silu(causal_dwconv(qkv, conv_w)).astype(jnp.float32)
    q, k, v = jnp.split(qkv, [qk_w, 2 * qk_w], axis=-1)
    q = l2_norm(q.reshape(bsz, s, h, dk)) * (dk ** -0.5)
    k = l2_norm(k.reshape(bsz, s, h, dk))
    v = v.reshape(bsz, s, h, dv)
    beta = jax.nn.sigmoid(b_logit.astype(jnp.float32))
    g = -jnp.exp(a_log.astype(jnp.float32)) * jax.nn.softplus((a_logit + dt_bias).astype(jnp.float32))
    o = gated_delta_chunked(q, k, v, beta, g).astype(x.dtype)
    o = rms_norm(o, norm_g) * jax.nn.silu(gate.reshape(bsz, s, h, dv))
    return o.reshape(bsz, s, v_w) @ w_out


def ssd_chunked(xdt, a, bm, cm):
    bsz, s, h, p = xdt.shape
    g, n_state = bm.shape[2], bm.shape[3]
    j = h // g
    c = SSD_CHUNK
    n = s // c
    xs = (jnp.moveaxis(xdt.reshape(bsz, n, c, g, j, p), 1, 0),
          jnp.moveaxis(a.reshape(bsz, n, c, g, j), 1, 0),
          jnp.moveaxis(bm.reshape(bsz, n, c, g, n_state), 1, 0),
          jnp.moveaxis(cm.reshape(bsz, n, c, g, n_state), 1, 0))
    causal = jnp.tril(jnp.ones((c, c), dtype=bool))

    def step(state, inp):
        xc, ac, bc, cc = inp
        acum = jnp.cumsum(ac, axis=1)
        acum_t = acum.transpose(0, 2, 3, 1)
        lmat = jnp.exp(jnp.where(causal, acum_t[..., :, None] - acum_t[..., None, :], -jnp.inf))
        cb = jnp.einsum("btgn,bsgn->bgts", cc, bc)
        y = jnp.einsum("bgts,bgjts,bsgjp->btgjp", cb, lmat, xc)
        y = y + jnp.einsum("btgn,bgjpn->btgjp", cc, state) * jnp.exp(acum)[..., None]
        last = acum_t[..., -1]
        state = state * jnp.exp(last)[..., None, None] + jnp.einsum(
            "bsgn,bgjs,bsgjp->bgjpn", bc, jnp.exp(last[..., None] - acum_t), xc)
        return state, y

    state0 = jnp.zeros((bsz, g, j, p, n_state), jnp.float32)
    _, ys = lax.scan(step, state0, xs)
    return jnp.moveaxis(ys, 0, 1).reshape(bsz, s, h, p)


def ssd_mixer(x, w_in, conv_w, conv_b, dt_bias, a_log, d_skip, norm_g, w_out):
    bsz, s, _ = x.shape
    h, p, g, n_state = SSD_HEADS, SSD_HEAD_DIM, SSD_GROUPS, SSD_STATE
    z, xbc, dt = jnp.split(x @ w_in, [SSD_INNER, 2 * SSD_INNER + 2 * g * n_state], axis=-1)
    xbc = jax.nn.silu(causal_dwconv(xbc, conv_w) + conv_b)
    xs, bm, cm = jnp.split(xbc.astype(jnp.float32), [SSD_INNER, SSD_INNER + g * n_state], axis=-1)
    xs = xs.reshape(bsz, s, h, p)
    bm = bm.reshape(bsz, s, g, n_state)
    cm = cm.reshape(bsz, s, g, n_state)
    dt = jax.nn.softplus((dt + dt_bias).astype(jnp.float32))
    a = dt * (-jnp.exp(a_log.astype(jnp.float32)))
    y = ssd_chunked(xs * dt[..., None], a, bm, cm) + xs * d_skip.astype(jnp.float32)[:, None]
    y = y.reshape(bsz, s, SSD_INNER).astype(x.dtype) * jax.nn.silu(z)
    y = rms_norm(y.reshape(bsz, s, g, SSD_INNER // g), norm_g.reshape(g, SSD_INNER // g)).reshape(bsz, s, SSD_INNER)
    return y @ w_out


def swiglu(x, w_gate, w_up, w_down):
    return (jax.nn.silu(x @ w_gate) * (x @ w_up)) @ w_down


def moe_swiglu(x, w_router, w_gate, w_up, w_down):
    logits = (x @ w_router).astype(jnp.float32)
    top_vals, top_idx = lax.top_k(logits, TOP_K)
    weights = jax.nn.softmax(top_vals, axis=-1)
    combine = jnp.sum(jax.nn.one_hot(top_idx, N_EXPERTS, dtype=jnp.float32) * weights[..., None], axis=-2).astype(x.dtype)
    out = jnp.zeros_like(x)
    for e in range(N_EXPERTS):
        out = out + combine[..., e:e + 1] * swiglu(x, w_gate[e], w_up[e], w_down[e])
    return out


def _normal(k, shape, fan_in, scale=1.0):
    return jax.random.normal(k, shape, jnp.float32) * (scale * fan_in ** -0.5)


def _dt_bias(k, shape):
    dt = jnp.exp(jax.random.uniform(k, shape, jnp.float32, math.log(1e-3), math.log(1e-1)))
    return dt + jnp.log(-jnp.expm1(-dt))


def _a_log(k, shape):
    return jnp.log(jax.random.uniform(k, shape, jnp.float32, 1.0, 16.0))


def setup_inputs(seed: int = 0) -> dict:
    key = jax.random.key(seed)
    ks = iter(jax.random.split(key, 48))
    nA, nB, nC, nD = N_LAYERS_PER_MIXER
    d = D_MODEL
    gdn_in = 2 * GDN_HEADS * GDN_DK + 2 * GDN_HEADS * GDN_DV + 2 * GDN_HEADS
    gdn_conv_ch = 2 * GDN_HEADS * GDN_DK + GDN_HEADS * GDN_DV
    ssd_conv_ch = SSD_INNER + 2 * SSD_GROUPS * SSD_STATE
    ssd_in = SSD_INNER + ssd_conv_ch + SSD_HEADS
    inp = {}
    inp["x"] = jax.random.normal(next(ks), (BATCH, SEQ, d), jnp.float32)
    inp["p"] = jax.random.normal(next(ks), (DEPTH, BATCH, SEQ, PLE_DIM), jnp.float32)
    inp["ln_g"] = 1.0 + 0.02 * jax.random.normal(next(ks), (DEPTH, 2, d), jnp.float32)
    inp["ln_b"] = 0.02 * jax.random.normal(next(ks), (DEPTH, 2, d), jnp.float32)
    inp["gm_w_in"] = _normal(next(ks), (nA, d, 2 * GM_WIDTH), d)
    inp["gm_b_in"] = 0.02 * jax.random.normal(next(ks), (nA, 2 * GM_WIDTH), jnp.float32)
    inp["gm_ln_g"] = 1.0 + 0.02 * jax.random.normal(next(ks), (nA, GM_WIDTH), jnp.float32)
    inp["gm_ln_b"] = 0.02 * jax.random.normal(next(ks), (nA, GM_WIDTH), jnp.float32)
    inp["gm_w_s"] = _normal(next(ks), (nA, GM_GROUPS, GM_CHUNK, GM_CHUNK), GM_CHUNK)
    inp["gm_b_s"] = 1.0 + 0.1 * jax.random.normal(next(ks), (nA, GM_GROUPS, GM_CHUNK), jnp.float32)
    inp["gm_w_out"] = _normal(next(ks), (nA, GM_WIDTH, d), GM_WIDTH, BETA_INIT)
    inp["fox_w_in"] = _normal(next(ks), (nB, d, 3 * d + FOX_HEADS), d)
    inp["fox_b_f"] = jax.random.uniform(next(ks), (nB, FOX_HEADS), jnp.float32, 1.0, 4.0)
    inp["fox_w_out"] = _normal(next(ks), (nB, d, d), d, BETA_INIT)
    inp["gdn_w_in"] = _normal(next(ks), (nC, d, gdn_in), d)
    inp["gdn_conv_w"] = _normal(next(ks), (nC, GDN_CONV, gdn_conv_ch), GDN_CONV)
    inp["gdn_a_log"] = _a_log(next(ks), (nC, GDN_HEADS))
    inp["gdn_dt_bias"] = _dt_bias(next(ks), (nC, GDN_HEADS))
    inp["gdn_norm_g"] = 1.0 + 0.02 * jax.random.normal(next(ks), (nC, GDN_DV), jnp.float32)
    inp["gdn_w_out"] = _normal(next(ks), (nC, GDN_HEADS * GDN_DV, d), GDN_HEADS * GDN_DV, BETA_INIT)
    inp["ssd_w_in"] = _normal(next(ks), (nD, d, ssd_in), d)
    inp["ssd_conv_w"] = _normal(next(ks), (nD, SSD_CONV, ssd_conv_ch), SSD_CONV)
    inp["ssd_conv_b"] = 0.02 * jax.random.normal(next(ks), (nD, ssd_conv_ch), jnp.float32)
    inp["ssd_dt_bias"] = _dt_bias(next(ks), (nD, SSD_HEADS))
    inp["ssd_a_log"] = _a_log(next(ks), (nD, SSD_HEADS))
    inp["ssd_d"] = 1.0 + 0.1 * jax.random.normal(next(ks), (nD, SSD_HEADS), jnp.float32)
    inp["ssd_norm_g"] = 1.0 + 0.02 * jax.random.normal(next(ks), (nD, SSD_INNER), jnp.float32)
    inp["ssd_w_out"] = _normal(next(ks), (nD, SSD_INNER, d), SSD_INNER, BETA_INIT)
    inp["ffn_w_gate"] = _normal(next(ks), (N_DENSE, d, D_FF), d)
    inp["ffn_w_up"] = _normal(next(ks), (N_DENSE, d, D_FF), d)
    inp["ffn_w_down"] = _normal(next(ks), (N_DENSE, D_FF, d), D_FF, BETA_INIT)
    inp["moe_w_router"] = _normal(next(ks), (N_MOE, d, N_EXPERTS), d)
    inp["moe_w_gate"] = _normal(next(ks), (N_MOE, N_EXPERTS, d, D_FF), d)
    inp["moe_w_up"] = _normal(next(ks), (N_MOE, N_EXPERTS, d, D_FF), d)
    inp["moe_w_down"] = _normal(next(ks), (N_MOE, N_EXPERTS, D_FF, d), D_FF, BETA_INIT)
    inp["ple_w_gate"] = _normal(next(ks), (DEPTH, d, d), d)
    inp["ple_b_gate"] = 0.02 * jax.random.normal(next(ks), (DEPTH, d), jnp.float32)
    inp["ple_w_proj"] = _normal(next(ks), (DEPTH, PLE_DIM, d), PLE_DIM)
    return inp


def reference(x, p, ln_g, ln_b, gm_w_in, gm_b_in, gm_ln_g, gm_ln_b, gm_w_s, gm_b_s, gm_w_out,
              fox_w_in, fox_b_f, fox_w_out, gdn_w_in, gdn_conv_w, gdn_a_log, gdn_dt_bias, gdn_norm_g, gdn_w_out,
              ssd_w_in, ssd_conv_w, ssd_conv_b, ssd_dt_bias, ssd_a_log, ssd_d, ssd_norm_g, ssd_w_out,
              ffn_w_gate, ffn_w_up, ffn_w_down, moe_w_router, moe_w_gate, moe_w_up, moe_w_down,
              ple_w_gate, ple_b_gate, ple_w_proj):
    h = x
    for i in range(DEPTH):
        m, j = i % N_MIXERS, i // N_MIXERS
        if m == 0:
            y = gmlp_mixer(h, gm_w_in[j], gm_b_in[j], gm_ln_g[j], gm_ln_b[j], gm_w_s[j], gm_b_s[j], gm_w_out[j])
        elif m == 1:
            y = fox_mixer(h, fox_w_in[j], fox_b_f[j], fox_w_out[j])
        elif m == 2:
            y = gdn_mixer(h, gdn_w_in[j], gdn_conv_w[j], gdn_a_log[j], gdn_dt_bias[j], gdn_norm_g[j], gdn_w_out[j])
        else:
            y = ssd_mixer(h, ssd_w_in[j], ssd_conv_w[j], ssd_conv_b[j], ssd_dt_bias[j], ssd_a_log[j], ssd_d[j],
                          ssd_norm_g[j], ssd_w_out[j])
        h = layer_norm(ALPHA * h + y, ln_g[i, 0], ln_b[i, 0])
        if i % 2 == 0:
            f = swiglu(h, ffn_w_gate[i // 2], ffn_w_up[i // 2], ffn_w_down[i // 2])
        else:
            f = moe_swiglu(h, moe_w_router[i // 2], moe_w_gate[i // 2], moe_w_up[i // 2], moe_w_down[i // 2])
        h = layer_norm(ALPHA * h + f, ln_g[i, 1], ln_b[i, 1])
        h = h + jax.nn.sigmoid(h @ ple_w_gate[i] + ple_b_gate[i]) * (p[i] @ ple_w_proj[i])
    return h
```

```python
import functools
import math

import jax
import jax.numpy as jnp
from jax import lax
from jax.experimental import pallas as pl
from jax.experimental.pallas import tpu as pltpu

F32 = jnp.float32
BF16 = jnp.bfloat16
HIGHEST = lax.Precision.HIGHEST

DEPTH = 4
ALPHA = (2 * DEPTH) ** 0.25
LN_EPS = 1e-5
RMS_EPS = 1e-6
L2_EPS = 1e-6

GM_CHUNK = 128
GM_GROUPS = 8
FOX_HEADS = 16
FOX_HEAD_DIM = 64
GDN_HEADS = 8
GDN_DK = 128
GDN_CHUNK = 64
SSD_HEADS = 32
SSD_HEAD_DIM = 64
SSD_GROUPS = 4
SSD_STATE = 128
SSD_CHUNK = 64
N_EXPERTS = 8
LANES = 128

VMEM_LIMIT_BYTES = 56 * 1024 * 1024


def _params(*sem):
    return pltpu.CompilerParams(dimension_semantics=sem, vmem_limit_bytes=VMEM_LIMIT_BYTES)


def _dot(a, b):
    return jnp.dot(a, b, preferred_element_type=F32)


def _dot_nt(a, b):
    return lax.dot_general(a, b, (((1,), (1,)), ((), ())), preferred_element_type=F32)


def _dot_tn(a, b):
    return lax.dot_general(a, b, (((0,), (0,)), ((), ())), preferred_element_type=F32)


def _dot_hi(a, b):
    return jnp.dot(a, b, preferred_element_type=F32, precision=HIGHEST)


def _layer_norm(r, g, b):
    mu = jnp.mean(r, axis=-1, keepdims=True)
    d = r - mu
    var = jnp.mean(d * d, axis=-1, keepdims=True)
    return d * lax.rsqrt(var + LN_EPS) * g + b


def _sigmoid(x):
    return 1.0 / (1.0 + jnp.exp(-x))


def _silu(x):
    return x * _sigmoid(x)


def _softplus(x):
    return jnp.maximum(x, 0.0) + jnp.log1p(jnp.exp(-jnp.abs(x)))


def _tril(n, strict=False):
    r = lax.broadcasted_iota(jnp.int32, (n, n), 0)
    c = lax.broadcasted_iota(jnp.int32, (n, n), 1)
    return (r > c) if strict else (r >= c)


def _linear_body(x_ref, w_ref, o_ref, *, hi):
    if hi:
        y = _dot_hi(x_ref[...], w_ref[...])
    else:
        y = _dot(x_ref[...].astype(BF16), w_ref[...])
    o_ref[...] = y.astype(o_ref.dtype)


def _linear(x, w, *, hi=False, out_dtype=BF16, tm=1024, tn=512):
    t, k = x.shape
    n = w.shape[1]
    tm, tn = min(tm, t), min(tn, n)
    return pl.pallas_call(
        functools.partial(_linear_body, hi=hi),
        grid=(t // tm, n // tn),
        in_specs=[pl.BlockSpec((tm, k), lambda i, j: (i, 0)), pl.BlockSpec((k, tn), lambda i, j: (0, j))],
        out_specs=pl.BlockSpec((tm, tn), lambda i, j: (i, j)),
        out_shape=jax.ShapeDtypeStruct((t, n), out_dtype),
        compiler_params=_params("parallel", "arbitrary"),
        name="linear_hi" if hi else "linear",
    )(x, w)


def _linear_gelu_body(x_ref, w_ref, b_ref, o_ref):
    y = _dot(x_ref[...].astype(BF16), w_ref[...]) + b_ref[...]
    o_ref[...] = jax.nn.gelu(y).astype(o_ref.dtype)


def _linear_gelu(x, w, b, *, tm=1024, tn=512):
    t, k = x.shape
    n = w.shape[1]
    tm = min(tm, t)
    return pl.pallas_call(
        _linear_gelu_body,
        grid=(t // tm, n // tn),
        in_specs=[pl.BlockSpec((tm, k), lambda i, j: (i, 0)), pl.BlockSpec((k, tn), lambda i, j: (0, j)),
                  pl.BlockSpec((1, tn), lambda i, j: (0, j))],
        out_specs=pl.BlockSpec((tm, tn), lambda i, j: (i, j)),
        out_shape=jax.ShapeDtypeStruct((t, n), BF16),
        compiler_params=_params("parallel", "arbitrary"),
        name="linear_gelu",
    )(x, w, b.reshape(1, n))


def _proj_ln_body(y_ref, w_ref, h_ref, g_ref, b_ref, o_ref):
    acc = _dot(y_ref[...].astype(BF16), w_ref[...])
    o_ref[...] = _layer_norm(ALPHA * h_ref[...] + acc, g_ref[...], b_ref[...])


def _proj_ln(y, w, h, g, b, *, tm=512):
    t, k = y.shape
    d = w.shape[1]
    tm = min(tm, t)
    row = lambda i: (i, 0)
    const = lambda i: (0, 0)
    return pl.pallas_call(
        _proj_ln_body,
        grid=(t // tm,),
        in_specs=[pl.BlockSpec((tm, k), row), pl.BlockSpec((k, d), const), pl.BlockSpec((tm, d), row),
                  pl.BlockSpec((1, d), const), pl.BlockSpec((1, d), const)],
        out_specs=pl.BlockSpec((tm, d), row),
        out_shape=jax.ShapeDtypeStruct((t, d), F32),
        compiler_params=_params("parallel"),
        name="proj_ln",
    )(y, w, h, g.reshape(1, d), b.reshape(1, d))


def _ffn_body(x_ref, wg_ref, wu_ref, wd_ref, g_ref, b_ref, o_ref, acc_ref):
    j = pl.program_id(1)

    @pl.when(j == 0)
    def _():
        acc_ref[...] = jnp.zeros_like(acc_ref)

    x = x_ref[...].astype(BF16)
    a = _dot(x, wg_ref[...])
    u = _dot(x, wu_ref[...])
    acc_ref[...] += _dot((_silu(a) * u).astype(BF16), wd_ref[...])

    @pl.when(j == pl.num_programs(1) - 1)
    def _():
        o_ref[...] = _layer_norm(ALPHA * x_ref[...] + acc_ref[...], g_ref[...], b_ref[...])


def _ffn(x, wg, wu, wd, g, b, *, tm=1024, tf=512):
    t, d = x.shape
    ff = wg.shape[1]
    tm = min(tm, t)
    return pl.pallas_call(
        _ffn_body,
        grid=(t // tm, ff // tf),
        in_specs=[pl.BlockSpec((tm, d), lambda i, j: (i, 0)), pl.BlockSpec((d, tf), lambda i, j: (0, j)),
                  pl.BlockSpec((d, tf), lambda i, j: (0, j)), pl.BlockSpec((tf, d), lambda i, j: (j, 0)),
                  pl.BlockSpec((1, d), lambda i, j: (0, 0)), pl.BlockSpec((1, d), lambda i, j: (0, 0))],
        out_specs=pl.BlockSpec((tm, d), lambda i, j: (i, 0)),
        out_shape=jax.ShapeDtypeStruct((t, d), F32),
        scratch_shapes=[pltpu.VMEM((tm, d), F32)],
        compiler_params=_params("parallel", "arbitrary"),
        name="ffn",
    )(x, wg, wu, wd, g.reshape(1, d), b.reshape(1, d))


def _router_body(x_ref, w_ref, o_ref):
    logits = _dot_hi(x_ref[...], w_ref[...])
    lane = lax.broadcasted_iota(jnp.int32, logits.shape, 1)
    neg = jnp.float32(-jnp.inf)
    logits = jnp.where(lane < N_EXPERTS, logits, neg)
    m1 = jnp.max(logits, axis=-1, keepdims=True)
    i1 = jnp.min(jnp.where(logits == m1, lane, LANES), axis=-1, keepdims=True)
    rest = jnp.where(lane == i1, neg, logits)
    m2 = jnp.max(rest, axis=-1, keepdims=True)
    i2 = jnp.min(jnp.where(rest == m2, lane, LANES), axis=-1, keepdims=True)
    e2 = jnp.exp(m2 - m1)
    w1 = 1.0 / (1.0 + e2)
    w2 = e2 / (1.0 + e2)
    o_ref[...] = jnp.where(lane == i1, w1, 0.0) + jnp.where(lane == i2, w2, 0.0)


def _router(x, w_router, *, tm=1024):
    t, d = x.shape
    tm = min(tm, t)
    w = jnp.pad(w_router, ((0, 0), (0, LANES - w_router.shape[1])))
    return pl.pallas_call(
        _router_body,
        grid=(t // tm,),
        in_specs=[pl.BlockSpec((tm, d), lambda i: (i, 0)), pl.BlockSpec((d, LANES), lambda i: (0, 0))],
        out_specs=pl.BlockSpec((tm, LANES), lambda i: (i, 0)),
        out_shape=jax.ShapeDtypeStruct((t, LANES), F32),
        compiler_params=_params("parallel"),
        name="router",
    )(x, w)


def _moe_body(x_ref, c_ref, wg_ref, wu_ref, wd_ref, g_ref, b_ref, o_ref, acc_ref):
    e, j = pl.program_id(1), pl.program_id(2)

    @pl.when((e == 0) & (j == 0))
    def _():
        acc_ref[...] = jnp.zeros_like(acc_ref)

    comb = c_ref[...]
    lane = lax.broadcasted_iota(jnp.int32, comb.shape, 1)
    ce = jnp.sum(jnp.where(lane == e, comb, 0.0), axis=-1, keepdims=True)
    x = x_ref[...].astype(BF16)
    a = _dot(x, wg_ref[0])
    u = _dot(x, wu_ref[0])
    acc_ref[...] += _dot((_silu(a) * u).astype(BF16), wd_ref[0]) * ce

    @pl.when((e == pl.num_programs(1) - 1) & (j == pl.num_programs(2) - 1))
    def _():
        o_ref[...] = _layer_norm(ALPHA * x_ref[...] + acc_ref[...], g_ref[...], b_ref[...])


def _moe(x, comb, wg, wu, wd, g, b, *, tm=1024, tf=512):
    t, d = x.shape
    ne, _, ff = wg.shape
    tm = min(tm, t)
    return pl.pallas_call(
        _moe_body,
        grid=(t // tm, ne, ff // tf),
        in_specs=[pl.BlockSpec((tm, d), lambda i, e, j: (i, 0)), pl.BlockSpec((tm, LANES), lambda i, e, j: (i, 0)),
                  pl.BlockSpec((1, d, tf), lambda i, e, j: (e, 0, j)), pl.BlockSpec((1, d, tf), lambda i, e, j: (e, 0, j)),
                  pl.BlockSpec((1, tf, d), lambda i, e, j: (e, j, 0)),
                  pl.BlockSpec((1, d), lambda i, e, j: (0, 0)), pl.BlockSpec((1, d), lambda i, e, j: (0, 0))],
        out_specs=pl.BlockSpec((tm, d), lambda i, e, j: (i, 0)),
        out_shape=jax.ShapeDtypeStruct((t, d), F32),
        scratch_shapes=[pltpu.VMEM((tm, d), F32)],
        compiler_params=_params("parallel", "arbitrary", "arbitrary"),
        name="moe",
    )(x, comb, wg, wu, wd, g.reshape(1, d), b.reshape(1, d))


def _moe_layer(h, w_router, wg, wu, wd, g, b):
    comb = _router(h, w_router)
    return _moe(h, comb, wg.astype(BF16), wu.astype(BF16), wd.astype(BF16), g, b)


def _ple_body(h_ref, p_ref, wg_ref, bg_ref, wp_ref, o_ref):
    h = h_ref[...]
    gate = _sigmoid(_dot(h.astype(BF16), wg_ref[...]) + bg_ref[...])
    o_ref[...] = h + gate * _dot(p_ref[...].astype(BF16), wp_ref[...])


def _ple(h, p, wg, bg, wp, *, tm=1024):
    t, d = h.shape
    pd = p.shape[1]
    tm = min(tm, t)
    row = lambda i: (i, 0)
    const = lambda i: (0, 0)
    return pl.pallas_call(
        _ple_body,
        grid=(t // tm,),
        in_specs=[pl.BlockSpec((tm, d), row), pl.BlockSpec((tm, pd), row), pl.BlockSpec((d, d), const),
                  pl.BlockSpec((1, d), const), pl.BlockSpec((pd, d), const)],
        out_specs=pl.BlockSpec((tm, d), row),
        out_shape=jax.ShapeDtypeStruct((t, d), F32),
        compiler_params=_params("parallel"),
        name="ple",
    )(h, p, wg, bg.reshape(1, d), wp)


def _gmlp_gate_body(u_ref, v_ref, lng_ref, lnb_ref, ws_ref, bs_ref, wo_ref, h_ref, g_ref, b_ref, o_ref, gated_ref):
    tm, width = u_ref.shape
    gw = width // GM_GROUPS
    v = _layer_norm(v_ref[...].astype(F32), lng_ref[...], lnb_ref[...]).astype(BF16)
    mask = _tril(GM_CHUNK)
    for g in range(GM_GROUPS):
        ws = jnp.where(mask, ws_ref[g], 0.0).astype(BF16)
        bias = bs_ref[:, g:g + 1]
        for c in range(tm // GM_CHUNK):
            rows = slice(c * GM_CHUNK, (c + 1) * GM_CHUNK)
            cols = slice(g * gw, (g + 1) * gw)
            sv = _dot(ws, v[rows, cols]) + bias
            gated_ref[rows, cols] = (u_ref[rows, cols].astype(F32) * sv).astype(BF16)
    acc = _dot(gated_ref[...], wo_ref[...])
    o_ref[...] = _layer_norm(ALPHA * h_ref[...] + acc, g_ref[...], b_ref[...])


def _gmlp_mixer(h, w_in, b_in, ln_g, ln_b, w_s, b_s, w_out, g, b, *, tm=512):
    t, d = h.shape
    width = w_out.shape[0]
    tm = min(tm, t)
    z = _linear_gelu(h, w_in.astype(BF16), b_in)
    row = lambda i: (i, 0)
    const = lambda i: (0, 0)
    return pl.pallas_call(
        _gmlp_gate_body,
        grid=(t // tm,),
        in_specs=[pl.BlockSpec((tm, width), row), pl.BlockSpec((tm, width), lambda i: (i, 1)),
                  pl.BlockSpec((1, width), const), pl.BlockSpec((1, width), const),
                  pl.BlockSpec((GM_GROUPS, GM_CHUNK, GM_CHUNK), lambda i: (0, 0, 0)),
                  pl.BlockSpec((GM_CHUNK, GM_GROUPS), const), pl.BlockSpec((width, d), const),
                  pl.BlockSpec((tm, d), row), pl.BlockSpec((1, d), const), pl.BlockSpec((1, d), const)],
        out_specs=pl.BlockSpec((tm, d), row),
        out_shape=jax.ShapeDtypeStruct((t, d), F32),
        scratch_shapes=[pltpu.VMEM((tm, width), BF16)],
        compiler_params=_params("parallel"),
        name="gmlp_gate",
    )(z, z, ln_g.reshape(1, width), ln_b.reshape(1, width), w_s, b_s.T, w_out.astype(BF16), h,
      g.reshape(1, d), b.reshape(1, d))


def _fox_decay_body(f_ref, bf_ref, c_ref):
    s = f_ref.shape[1]
    x = f_ref[0] + bf_ref[...]
    log_f = jnp.minimum(x, 0.0) - jnp.log1p(jnp.exp(-jnp.abs(x)))
    tri = _tril(LANES).astype(F32)
    carry = jnp.zeros((1, LANES), F32)
    for i in range(s // LANES):
        cs = _dot_hi(tri, log_f[i * LANES:(i + 1) * LANES]) + carry
        c_ref[0, i * LANES:(i + 1) * LANES, :] = cs
        carry = cs[LANES - 1:LANES]


def _fox_decay(f, b_f):
    bsz, s, _ = f.shape
    blk = lambda i: (i, 0, 0)
    return pl.pallas_call(
        _fox_decay_body,
        grid=(bsz,),
        in_specs=[pl.BlockSpec((1, s, LANES), blk), pl.BlockSpec((1, LANES), lambda i: (0, 0))],
        out_specs=pl.BlockSpec((1, s, LANES), blk),
        out_shape=jax.ShapeDtypeStruct((bsz, s, LANES), F32),
        compiler_params=_params("parallel"),
        name="fox_decay",
    )(f, b_f)


def _fox_attn_body(q_ref, k_ref, v_ref, cq_ref, ck_ref, o_ref, m_ref, l_ref, acc_ref, *, scale):
    qi, kj = pl.program_id(2), pl.program_id(3)
    tq, tk = q_ref.shape[1], k_ref.shape[1]
    half = FOX_HEAD_DIM

    @pl.when(kj == 0)
    def _():
        m_ref[...] = jnp.full_like(m_ref, -1e30)
        l_ref[...] = jnp.zeros_like(l_ref)
        acc_ref[...] = jnp.zeros_like(acc_ref)

    @pl.when(kj <= qi)
    def _():
        q = q_ref[0] * scale
        k = k_ref[0]
        v = v_ref[0]
        lane = lax.broadcasted_iota(jnp.int32, q.shape, 1)
        row = qi * tq + lax.broadcasted_iota(jnp.int32, (tq, tk), 0)
        col = kj * tk + lax.broadcasted_iota(jnp.int32, (tq, tk), 1)
        causal = row >= col
        pv = []
        alphas = []
        for hh in range(2):
            qh = jnp.where((lane >= hh * half) & (lane < (hh + 1) * half), q, jnp.zeros_like(q))
            s = _dot_nt(qh, k) + cq_ref[0, 0, :, hh * half:hh * half + 1] - ck_ref[0, 0, hh:hh + 1, :]
            s = jnp.where(causal, s, -1e30)
            m_prev = m_ref[hh]
            m_new = jnp.maximum(m_prev, jnp.max(s, axis=-1, keepdims=True))
            alpha = jnp.exp(m_prev - m_new)
            p = jnp.exp(s - m_new)
            l_ref[hh] = alpha * l_ref[hh] + jnp.sum(p, axis=-1, keepdims=True)
            m_ref[hh] = m_new
            pv.append(_dot(p.astype(BF16), v))
            alphas.append(alpha)
        first = lax.broadcasted_iota(jnp.int32, acc_ref.shape, 1) < half
        acc_ref[...] = acc_ref[...] * jnp.where(first, alphas[0], alphas[1]) + jnp.where(first, pv[0], pv[1])

    @pl.when(kj == pl.num_programs(3) - 1)
    def _():
        first = lax.broadcasted_iota(jnp.int32, acc_ref.shape, 1) < half
        o_ref[0] = (acc_ref[...] / jnp.where(first, l_ref[0], l_ref[1])).astype(o_ref.dtype)


def _fox_attention(qkv, cq, ck, *, tq=512, tk=512):
    bsz, s, w3 = qkv.shape
    width = w3 // 3
    pairs = width // LANES
    tq, tk = min(tq, s), min(tk, s)
    return pl.pallas_call(
        functools.partial(_fox_attn_body, scale=FOX_HEAD_DIM ** -0.5),
        grid=(bsz, pairs, s // tq, s // tk),
        in_specs=[pl.BlockSpec((1, tq, LANES), lambda b, h, i, j: (b, i, h)),
                  pl.BlockSpec((1, tk, LANES), lambda b, h, i, j: (b, jnp.minimum(j, i), pairs + h)),
                  pl.BlockSpec((1, tk, LANES), lambda b, h, i, j: (b, jnp.minimum(j, i), 2 * pairs + h)),
                  pl.BlockSpec((1, 1, tq, LANES), lambda b, h, i, j: (b, h, i, 0)),
                  pl.BlockSpec((1, 1, 2, tk), lambda b, h, i, j: (b, h, 0, jnp.minimum(j, i)))],
        out_specs=pl.BlockSpec((1, tq, LANES), lambda b, h, i, j: (b, i, h)),
        out_shape=jax.ShapeDtypeStruct((bsz, s, width), BF16),
        scratch_shapes=[pltpu.VMEM((2, tq, 1), F32), pltpu.VMEM((2, tq, 1), F32), pltpu.VMEM((tq, LANES), F32)],
        compiler_params=_params("parallel", "parallel", "parallel", "arbitrary"),
        name="fox_attn",
    )(qkv, qkv, qkv, cq, ck)


def _fox_mixer(h, bsz, w_in, b_f, w_out, g, b):
    t, d = h.shape
    s = t // bsz
    width = FOX_HEADS * FOX_HEAD_DIM
    pairs = FOX_HEADS // 2
    qkv = _linear(h, w_in[:, :3 * width].astype(BF16))
    w_f = jnp.pad(w_in[:, 3 * width:], ((0, 0), (0, LANES - FOX_HEADS)))
    f = _linear(h, w_f, hi=True, out_dtype=F32, tn=LANES)
    b_pad = jnp.pad(b_f, (0, LANES - FOX_HEADS)).reshape(1, LANES)
    c = _fox_decay(f.reshape(bsz, s, LANES), b_pad)[:, :, :FOX_HEADS]
    ck = c.transpose(0, 2, 1).reshape(bsz, pairs, 2, s)
    cq = jnp.repeat(c.reshape(bsz, s, pairs, 2).transpose(0, 2, 1, 3), FOX_HEAD_DIM, axis=-1)
    o = _fox_attention(qkv.reshape(bsz, s, 3 * width), cq, ck)
    return _proj_ln(o.reshape(t, width), w_out.astype(BF16), h, g, b)


def _causal_conv(x, taps):
    k = len(taps)
    row = lax.broadcasted_iota(jnp.int32, x.shape, 0)
    out = x * taps[k - 1]
    for sh in range(1, k):
        shifted = jnp.where(row >= sh, pltpu.roll(x, sh, 0), 0.0)
        out = out + shifted * taps[k - 1 - sh]
    return out


def _gdn_gates_body(x_ref, dtb_ref, alog_ref, o_ref):
    x = x_ref[...]
    lane = lax.broadcasted_iota(jnp.int32, x.shape, 1)
    beta = _sigmoid(x)
    gdec = -jnp.exp(alog_ref[...]) * _softplus(x + dtb_ref[...])
    gdec = jnp.where((lane >= GDN_HEADS) & (lane < 2 * GDN_HEADS), gdec, 0.0)
    tri = _tril(GDN_CHUNK).astype(F32)
    lane_c = lax.broadcasted_iota(jnp.int32, (GDN_CHUNK, LANES), 1)
    for c in range(x.shape[0] // GDN_CHUNK):
        rows = slice(c * GDN_CHUNK, (c + 1) * GDN_CHUNK)
        gc = _dot_hi(tri, gdec[rows])
        o_ref[rows, :] = jnp.where(lane_c < GDN_HEADS, beta[rows], gc)


def _gdn_gates(x, dt_bias, a_log, *, tm=512):
    t = x.shape[0]
    tm = min(tm, t)
    pad = lambda v: jnp.pad(v, (GDN_HEADS, LANES - 2 * GDN_HEADS)).reshape(1, LANES)
    return pl.pallas_call(
        _gdn_gates_body,
        grid=(t // tm,),
        in_specs=[pl.BlockSpec((tm, LANES), lambda i: (i, 0)), pl.BlockSpec((1, LANES), lambda i: (0, 0)),
                  pl.BlockSpec((1, LANES), lambda i: (0, 0))],
        out_specs=pl.BlockSpec((tm, LANES), lambda i: (i, 0)),
        out_shape=jax.ShapeDtypeStruct((t, LANES), F32),
        compiler_params=_params("parallel"),
        name="gdn_gates",
    )(x, pad(dt_bias), pad(a_log))


def _unit_lower_inverse(m):
    n = m.shape[0]
    eye = (lax.broadcasted_iota(jnp.int32, (n, n), 0) == lax.broadcasted_iota(jnp.int32, (n, n), 1)).astype(F32)
    p = -m
    inv = eye + p
    step = 1
    while 2 * step < n:
        p = _dot_hi(p, p)
        inv = inv + _dot_hi(inv, p)
        step *= 2
    return inv


def _gdn_body(q_ref, k_ref, v_ref, gate_ref, cw_ref, gcol_ref, grow_ref, ng_ref, o_ref,
              q_s, k_s, kb_s, u_s, w_s, g_s, a_s, o_s):
    hd = pl.program_id(1)
    s = q_ref.shape[1]
    n_chunks = s // GDN_CHUNK
    c = GDN_CHUNK

    def conv_act(x_ref, which):
        taps = [cw_ref[0, which, i:i + 1, :] for i in range(cw_ref.shape[2])]
        return _silu(_causal_conv(x_ref[0].astype(F32), taps))

    q = conv_act(q_ref, 0)
    k = conv_act(k_ref, 1)
    v = conv_act(v_ref, 2)
    q = q * lax.rsqrt(jnp.sum(q * q, axis=-1, keepdims=True) + L2_EPS) * (GDN_DK ** -0.5)
    k = k * lax.rsqrt(jnp.sum(k * k, axis=-1, keepdims=True) + L2_EPS)
    gcols = gcol_ref[0]
    lane = lax.broadcasted_iota(jnp.int32, gcols.shape, 1)
    beta = jnp.sum(jnp.where(lane == hd, gcols, 0.0), axis=-1, keepdims=True)
    gc = jnp.sum(jnp.where(lane == hd + GDN_HEADS, gcols, 0.0), axis=-1, keepdims=True)
    kb = k * beta
    q_s[...] = q
    k_s[...] = k
    kb_s[...] = kb
    u_s[...] = v * beta
    w_s[...] = kb * jnp.exp(gc)
    g_s[...] = jnp.broadcast_to(gc, g_s.shape)
    causal = _tril(c)
    strict = _tril(c, strict=True)

    def prep(i, carry):
        rows = pl.ds(pl.multiple_of(i * c, c), c)
        kc = k_s[rows, :]
        gcc = g_s[rows, 0:1]
        grow = grow_ref[0, 0, pl.ds(i, 1), :]
        decay = jnp.exp(jnp.where(causal, gcc - grow, -jnp.inf))
        kk = _dot_nt(kb_s[rows, :], kc)
        inv = _unit_lower_inverse(jnp.where(strict, kk * decay, 0.0))
        a_s[i] = _dot_nt(q_s[rows, :], kc) * decay
        u_s[rows, :] = _dot_hi(inv, u_s[rows, :])
        w_s[rows, :] = _dot_hi(inv, w_s[rows, :])
        return carry

    lax.fori_loop(0, n_chunks, prep, 0)

    def scan(i, state):
        rows = pl.ds(pl.multiple_of(i * c, c), c)
        gcc = g_s[rows, 0:1]
        g_last = g_s[pl.ds(i * c + c - 1, 1), 0:1]
        v_new = u_s[rows, :] - _dot(w_s[rows, :], state)
        o_s[rows, :] = _dot(q_s[rows, :] * jnp.exp(gcc), state) + _dot(a_s[i], v_new)
        kd = k_s[rows, :] * jnp.exp(g_last - gcc)
        return state * jnp.exp(g_last) + _dot_tn(kd, v_new)

    lax.fori_loop(0, n_chunks, scan, jnp.zeros((GDN_DK, o_ref.shape[2]), F32))
    o = o_s[...]
    o = o * lax.rsqrt(jnp.mean(o * o, axis=-1, keepdims=True) + RMS_EPS) * ng_ref[...]
    o_ref[0] = (o * _silu(gate_ref[0].astype(F32))).astype(o_ref.dtype)


def _gdn_core(proj, conv_w, gcol, grow, norm_g):
    bsz, s, _ = proj.shape
    hd, dk, c = GDN_HEADS, GDN_DK, GDN_CHUNK
    n = s // c
    cw = conv_w.reshape(conv_w.shape[0], 3, hd, dk).transpose(2, 1, 0, 3)
    kk = conv_w.shape[0]
    seq = lambda off: pl.BlockSpec((1, s, dk), lambda b, h: (b, 0, off + h))
    return pl.pallas_call(
        _gdn_body,
        grid=(bsz, hd),
        in_specs=[seq(0), seq(hd), seq(2 * hd), seq(3 * hd),
                  pl.BlockSpec((1, 3, kk, dk), lambda b, h: (h, 0, 0, 0)),
                  pl.BlockSpec((1, s, LANES), lambda b, h: (b, 0, 0)),
                  pl.BlockSpec((1, 1, n, c), lambda b, h: (b, h, 0, 0)),
                  pl.BlockSpec((1, dk), lambda b, h: (0, 0))],
        out_specs=seq(0),
        out_shape=jax.ShapeDtypeStruct((bsz, s, hd * dk), BF16),
        scratch_shapes=[pltpu.VMEM((s, dk), F32)] * 5 + [pltpu.VMEM((s, LANES), F32), pltpu.VMEM((n, c, c), F32),
                                                          pltpu.VMEM((s, dk), F32)],
        compiler_params=_params("parallel", "arbitrary"),
        name="gdn_core",
    )(proj, proj, proj, proj, cw, gcol, grow, norm_g.reshape(1, dk))


def _gdn_mixer(h, bsz, w_in, conv_w, a_log, dt_bias, norm_g, w_out, g, b):
    t, d = h.shape
    s = t // bsz
    hd = GDN_HEADS
    qkv_w = 3 * hd * GDN_DK
    gate_w = hd * GDN_DK
    proj = _linear(h, w_in[:, :qkv_w + gate_w].astype(BF16))
    w_small = jnp.pad(w_in[:, qkv_w + gate_w:], ((0, 0), (0, LANES - 2 * hd)))
    logits = _linear(h, w_small, hi=True, out_dtype=F32, tn=LANES)
    gcol = _gdn_gates(logits, dt_bias, a_log).reshape(bsz, s, LANES)
    grow = gcol[:, :, hd:2 * hd].transpose(0, 2, 1).reshape(bsz, hd, s // GDN_CHUNK, GDN_CHUNK)
    o = _gdn_core(proj.reshape(bsz, s, qkv_w + gate_w), conv_w, gcol, grow, norm_g)
    return _proj_ln(o.reshape(t, gate_w), w_out.astype(BF16), h, g, b)


def _ssd_gates_body(x_ref, dtb_ref, alog_ref, o_ref):
    x = x_ref[...]
    lane = lax.broadcasted_iota(jnp.int32, x.shape, 1)
    dt = _softplus(x + dtb_ref[...])
    a = jnp.where(lane < SSD_HEADS, dt * (-jnp.exp(alog_ref[...])), 0.0)
    tri = _tril(SSD_CHUNK).astype(F32)
    lane_c = lax.broadcasted_iota(jnp.int32, (SSD_CHUNK, LANES), 1)
    for c in range(x.shape[0] // SSD_CHUNK):
        rows = slice(c * SSD_CHUNK, (c + 1) * SSD_CHUNK)
        acum = pltpu.roll(_dot_hi(tri, a[rows]), SSD_HEADS, 1)
        o_ref[rows, :] = jnp.where(lane_c < SSD_HEADS, dt[rows], acum)


def _ssd_gates(x, dt_bias, a_log, *, tm=512):
    t = x.shape[0]
    tm = min(tm, t)
    pad = lambda v: jnp.pad(v, (0, LANES - SSD_HEADS)).reshape(1, LANES)
    return pl.pallas_call(
        _ssd_gates_body,
        grid=(t // tm,),
        in_specs=[pl.BlockSpec((tm, LANES), lambda i: (i, 0)), pl.BlockSpec((1, LANES), lambda i: (0, 0)),
                  pl.BlockSpec((1, LANES), lambda i: (0, 0))],
        out_specs=pl.BlockSpec((tm, LANES), lambda i: (i, 0)),
        out_shape=jax.ShapeDtypeStruct((t, LANES), F32),
        compiler_params=_params("parallel"),
        name="ssd_gates",
    )(x, pad(dt_bias), pad(a_log))


def _ssd_body(x_ref, b_ref, c_ref, z_ref, cwx_ref, cwb_ref, cwc_ref, cbx_ref, cbb_ref, cbc_ref,
              gcol_ref, arow_ref, d_ref, ng_ref, o_ref, x_s, b_s, c_s, y_s):
    grp = pl.program_id(1)
    s = x_ref.shape[1]
    c = SSD_CHUNK
    hp = SSD_HEAD_DIM
    heads = x_ref.shape[2] // hp
    n_chunks = s // c

    def conv_act(ref, w_ref, bias_ref):
        taps = [w_ref[i:i + 1, :] for i in range(w_ref.shape[0])]
        return _silu(_causal_conv(ref[0].astype(F32), taps) + bias_ref[...])

    x_s[...] = conv_act(x_ref, cwx_ref, cbx_ref)
    b_s[...] = conv_act(b_ref, cwb_ref, cbb_ref)
    c_s[...] = conv_act(c_ref, cwc_ref, cbc_ref)
    causal = _tril(c)
    pair_lane = lax.broadcasted_iota(jnp.int32, (c, 2 * hp), 1) < hp

    def head_cols(gcol, base):
        lane = lax.broadcasted_iota(jnp.int32, gcol.shape, 1)
        return [jnp.sum(jnp.where(lane == base + grp * heads + j, gcol, 0.0), axis=-1, keepdims=True)
                for j in range(heads)]

    def widen(cols):
        pairs = [jnp.where(pair_lane, cols[2 * p], cols[2 * p + 1]) for p in range(heads // 2)]
        return jnp.concatenate(pairs, axis=1)

    def step(i, state):
        rows = pl.ds(pl.multiple_of(i * c, c), c)
        gcol = gcol_ref[0, rows, :]
        dt = head_cols(gcol, 0)
        acum = head_cols(gcol, SSD_HEADS)
        arow = arow_ref[0, 0, i]
        xc = x_s[rows, :]
        bc = b_s[rows, :]
        cc = c_s[rows, :]
        xdt = xc * widen(dt)
        cb = _dot_nt(cc, bc)
        y = _dot(cc, state) * widen([jnp.exp(a) for a in acum])
        intra = []
        for p in range(heads // 2):
            xp = xdt[:, 2 * p * hp:(2 * p + 2) * hp]
            ys = []
            for j in (2 * p, 2 * p + 1):
                lmat = jnp.exp(jnp.where(causal, acum[j] - arow[j:j + 1, :], -jnp.inf))
                ys.append(_dot(cb * lmat, xp))
            intra.append(jnp.where(pair_lane, ys[0], ys[1]))
        y = y + jnp.concatenate(intra, axis=1) + xc * d_ref[...]
        y_s[rows, :] = y
        last = head_cols(gcol_ref[0, pl.ds(i * c + c - 1, 1), :], SSD_HEADS)
        xdec = xdt * widen([jnp.exp(last[j] - acum[j]) for j in range(heads)])
        first_lane = lax.broadcasted_iota(jnp.int32, (1, 2 * hp), 1) < hp
        lastw = jnp.concatenate([jnp.where(first_lane, jnp.exp(last[2 * p]), jnp.exp(last[2 * p + 1]))
                                 for p in range(heads // 2)], axis=1)
        return state * lastw + _dot_tn(bc, xdec)

    lax.fori_loop(0, n_chunks, step, jnp.zeros((SSD_STATE, x_ref.shape[2]), F32))
    y = y_s[...] * _silu(z_ref[0].astype(F32))
    y = y * lax.rsqrt(jnp.mean(y * y, axis=-1, keepdims=True) + RMS_EPS) * ng_ref[...]
    o_ref[0] = y.astype(o_ref.dtype)


def _ssd_core(proj, conv_w, conv_b, gcol, arow, d_wide, norm_g):
    bsz, s, _ = proj.shape
    grp, n_state, c = SSD_GROUPS, SSD_STATE, SSD_CHUNK
    inner = SSD_HEADS * SSD_HEAD_DIM
    gw = inner // grp
    heads = SSD_HEADS // grp
    n = s // c
    kk = conv_w.shape[0]
    xblk = lambda off: pl.BlockSpec((1, s, gw), lambda b, g: (b, 0, off + g))
    nblk = lambda off: pl.BlockSpec((1, s, n_state), lambda b, g: (b, 0, off + g))
    wx = lambda rows, off: pl.BlockSpec((rows, gw), lambda b, g: (0, off + g))
    wn = lambda rows, off: pl.BlockSpec((rows, n_state), lambda b, g: (0, off + g))
    x_off = inner // gw
    b_off = 2 * inner // n_state
    c_off = b_off + grp
    return pl.pallas_call(
        _ssd_body,
        grid=(bsz, grp),
        in_specs=[xblk(x_off), nblk(b_off), nblk(c_off), xblk(0),
                  wx(kk, 0), wn(kk, inner // n_state), wn(kk, inner // n_state + grp),
                  wx(1, 0), wn(1, inner // n_state), wn(1, inner // n_state + grp),
                  pl.BlockSpec((1, s, LANES), lambda b, g: (b, 0, 0)),
                  pl.BlockSpec((1, 1, n, heads, c), lambda b, g: (b, g, 0, 0, 0)),
                  wx(1, 0), wx(1, 0)],
        out_specs=xblk(0),
        out_shape=jax.ShapeDtypeStruct((bsz, s, inner), BF16),
        scratch_shapes=[pltpu.VMEM((s, gw), F32), pltpu.VMEM((s, n_state), F32), pltpu.VMEM((s, n_state), F32),
                        pltpu.VMEM((s, gw), F32)],
        compiler_params=_params("parallel", "arbitrary"),
        name="ssd_core",
    )(proj, proj, proj, proj, conv_w, conv_w, conv_w, conv_b, conv_b, conv_b, gcol, arow, d_wide, norm_g)


def _ssd_mixer(h, bsz, w_in, conv_w, conv_b, dt_bias, a_log, d_skip, norm_g, w_out, g, b):
    t, d = h.shape
    s = t // bsz
    inner = SSD_HEADS * SSD_HEAD_DIM
    main = 2 * inner + 2 * SSD_GROUPS * SSD_STATE
    heads = SSD_HEADS // SSD_GROUPS
    proj = _linear(h, w_in[:, :main].astype(BF16))
    w_dt = jnp.pad(w_in[:, main:], ((0, 0), (0, LANES - SSD_HEADS)))
    dt_logits = _linear(h, w_dt, hi=True, out_dtype=F32, tn=LANES)
    gcol = _ssd_gates(dt_logits, dt_bias, a_log).reshape(bsz, s, LANES)
    arow = gcol[:, :, SSD_HEADS:2 * SSD_HEADS].reshape(bsz, s // SSD_CHUNK, SSD_CHUNK, SSD_GROUPS, heads)
    arow = arow.transpose(0, 3, 1, 4, 2)
    d_wide = jnp.repeat(d_skip, SSD_HEAD_DIM).reshape(1, inner)
    o = _ssd_core(proj.reshape(bsz, s, main), conv_w, conv_b.reshape(1, -1), gcol, arow, d_wide,
                  norm_g.reshape(1, inner))
    return _proj_ln(o.reshape(t, inner), w_out.astype(BF16), h, g, b)


def kernel(x, p, ln_g, ln_b, gm_w_in, gm_b_in, gm_ln_g, gm_ln_b, gm_w_s, gm_b_s, gm_w_out, fox_w_in, fox_b_f, fox_w_out, gdn_w_in, gdn_conv_w, gdn_a_log, gdn_dt_bias, gdn_norm_g, gdn_w_out, ssd_w_in, ssd_conv_w, ssd_conv_b, ssd_dt_bias, ssd_a_log, ssd_d, ssd_norm_g, ssd_w_out, ffn_w_gate, ffn_w_up, ffn_w_down, moe_w_router, moe_w_gate, moe_w_up, moe_w_down, ple_w_gate, ple_b_gate, ple_w_proj):
    bsz, s, d = x.shape
    t = bsz * s
    h = x.reshape(t, d)
    pf = p.reshape(p.shape[0], t, p.shape[-1])
    for i in range(DEPTH):
        m, j = i % 4, i // 4
        g0, b0 = ln_g[i, 0], ln_b[i, 0]
        if m == 0:
            h = _gmlp_mixer(h, gm_w_in[j], gm_b_in[j], gm_ln_g[j], gm_ln_b[j], gm_w_s[j], gm_b_s[j], gm_w_out[j], g0, b0)
        elif m == 1:
            h = _fox_mixer(h, bsz, fox_w_in[j], fox_b_f[j], fox_w_out[j], g0, b0)
        elif m == 2:
            h = _gdn_mixer(h, bsz, gdn_w_in[j], gdn_conv_w[j], gdn_a_log[j], gdn_dt_bias[j], gdn_norm_g[j], gdn_w_out[j],
                           g0, b0)
        else:
            h = _ssd_mixer(h, bsz, ssd_w_in[j], ssd_conv_w[j], ssd_conv_b[j], ssd_dt_bias[j], ssd_a_log[j], ssd_d[j],
                           ssd_norm_g[j], ssd_w_out[j], g0, b0)
        g1, b1 = ln_g[i, 1], ln_b[i, 1]
        if i % 2 == 0:
            h = _ffn(h, ffn_w_gate[i // 2].astype(BF16), ffn_w_up[i // 2].astype(BF16), ffn_w_down[i // 2].astype(BF16),
                     g1, b1)
        else:
            h = _moe_layer(h, moe_w_router[i // 2], moe_w_gate[i // 2], moe_w_up[i // 2], moe_w_down[i // 2], g1, b1)
        h = _ple(h, pf[i], ple_w_gate[i].astype(BF16), ple_b_gate[i], ple_w_proj[i].astype(BF16))
    return h.reshape(bsz, s, d)
```

```python
import functools
import math

import jax
import jax.numpy as jnp
from jax import lax
from jax.experimental import pallas as pl
from jax.experimental.pallas import tpu as pltpu

F32 = jnp.float32
BF16 = jnp.bfloat16
HIGHEST = lax.Precision.HIGHEST

DEPTH = 4
ALPHA = (2 * DEPTH) ** 0.25
LN_EPS = 1e-5
RMS_EPS = 1e-6
L2_EPS = 1e-6

GM_CHUNK = 128
GM_GROUPS = 8
FOX_HEADS = 16
FOX_HEAD_DIM = 64
GDN_HEADS = 8
GDN_DK = 128
GDN_CHUNK = 64
SSD_HEADS = 32
SSD_HEAD_DIM = 64
SSD_GROUPS = 4
SSD_STATE = 128
SSD_CHUNK = 64
N_EXPERTS = 8
LANES = 128

VMEM_LIMIT_BYTES = 56 * 1024 * 1024


def _params(*sem):
    return pltpu.CompilerParams(dimension_semantics=sem, vmem_limit_bytes=VMEM_LIMIT_BYTES)


def _dot(a, b):
    return jnp.dot(a, b, preferred_element_type=F32)


def _dot_nt(a, b):
    return lax.dot_general(a, b, (((1,), (1,)), ((), ())), preferred_element_type=F32)


def _dot_tn(a, b):
    return lax.dot_general(a, b, (((0,), (0,)), ((), ())), preferred_element_type=F32)


def _dot_hi(a, b):
    return jnp.dot(a, b, preferred_element_type=F32, precision=HIGHEST)


def _split_bf16(a):
    hi = a.astype(BF16)
    return hi, (a - hi.astype(F32)).astype(BF16)


def _dot_split(a, b):
    ah, al = _split_bf16(a)
    bh, bl = _split_bf16(b)
    return _dot(ah, bh) + (_dot(ah, bl) + _dot(al, bh))


def _layer_norm(r, g, b):
    mu = jnp.mean(r, axis=-1, keepdims=True)
    d = r - mu
    var = jnp.mean(d * d, axis=-1, keepdims=True)
    return d * lax.rsqrt(var + LN_EPS) * g + b


def _sigmoid(x):
    return 1.0 / (1.0 + jnp.exp(-x))


def _silu(x):
    return x * _sigmoid(x)


def _softplus(x):
    return jnp.maximum(x, 0.0) + jnp.log1p(jnp.exp(-jnp.abs(x)))


def _tril(n, strict=False):
    r = lax.broadcasted_iota(jnp.int32, (n, n), 0)
    c = lax.broadcasted_iota(jnp.int32, (n, n), 1)
    return (r > c) if strict else (r >= c)


def _linear_body(x_ref, w_ref, o_ref, *, hi):
    if hi:
        y = _dot_hi(x_ref[...], w_ref[...])
    else:
        y = _dot(x_ref[...].astype(BF16), w_ref[...])
    o_ref[...] = y.astype(o_ref.dtype)


def _linear(x, w, *, hi=False, out_dtype=BF16, tm=1024, tn=512):
    t, k = x.shape
    n = w.shape[1]
    tm, tn = min(tm, t), min(tn, n)
    return pl.pallas_call(
        functools.partial(_linear_body, hi=hi),
        grid=(t // tm, n // tn),
        in_specs=[pl.BlockSpec((tm, k), lambda i, j: (i, 0)), pl.BlockSpec((k, tn), lambda i, j: (0, j))],
        out_specs=pl.BlockSpec((tm, tn), lambda i, j: (i, j)),
        out_shape=jax.ShapeDtypeStruct((t, n), out_dtype),
        compiler_params=_params("parallel", "arbitrary"),
        name="linear_hi" if hi else "linear",
    )(x, w)


def _linear_gelu_body(x_ref, w_ref, b_ref, o_ref):
    y = _dot(x_ref[...].astype(BF16), w_ref[...]) + b_ref[...]
    o_ref[...] = jax.nn.gelu(y).astype(o_ref.dtype)


def _linear_gelu(x, w, b, *, tm=1024, tn=512):
    t, k = x.shape
    n = w.shape[1]
    tm = min(tm, t)
    return pl.pallas_call(
        _linear_gelu_body,
        grid=(t // tm, n // tn),
        in_specs=[pl.BlockSpec((tm, k), lambda i, j: (i, 0)), pl.BlockSpec((k, tn), lambda i, j: (0, j)),
                  pl.BlockSpec((1, tn), lambda i, j: (0, j))],
        out_specs=pl.BlockSpec((tm, tn), lambda i, j: (i, j)),
        out_shape=jax.ShapeDtypeStruct((t, n), BF16),
        compiler_params=_params("parallel", "arbitrary"),
        name="linear_gelu",
    )(x, w, b.reshape(1, n))


def _proj_ln_body(y_ref, w_ref, h_ref, g_ref, b_ref, o_ref):
    acc = _dot(y_ref[...].astype(BF16), w_ref[...])
    o_ref[...] = _layer_norm(ALPHA * h_ref[...] + acc, g_ref[...], b_ref[...])


def _proj_ln(y, w, h, g, b, *, tm=512):
    t, k = y.shape
    d = w.shape[1]
    tm = min(tm, t)
    row = lambda i: (i, 0)
    const = lambda i: (0, 0)
    return pl.pallas_call(
        _proj_ln_body,
        grid=(t // tm,),
        in_specs=[pl.BlockSpec((tm, k), row), pl.BlockSpec((k, d), const), pl.BlockSpec((tm, d), row),
                  pl.BlockSpec((1, d), const), pl.BlockSpec((1, d), const)],
        out_specs=pl.BlockSpec((tm, d), row),
        out_shape=jax.ShapeDtypeStruct((t, d), F32),
        compiler_params=_params("parallel"),
        name="proj_ln",
    )(y, w, h, g.reshape(1, d), b.reshape(1, d))


def _ffn_body(x_ref, wg_ref, wu_ref, wd_ref, g_ref, b_ref, o_ref, acc_ref):
    j = pl.program_id(1)

    @pl.when(j == 0)
    def _():
        acc_ref[...] = jnp.zeros_like(acc_ref)

    x = x_ref[...].astype(BF16)
    a = _dot(x, wg_ref[...])
    u = _dot(x, wu_ref[...])
    acc_ref[...] += _dot((_silu(a) * u).astype(BF16), wd_ref[...])

    @pl.when(j == pl.num_programs(1) - 1)
    def _():
        o_ref[...] = _layer_norm(ALPHA * x_ref[...] + acc_ref[...], g_ref[...], b_ref[...])


def _ffn(x, wg, wu, wd, g, b, *, tm=1024, tf=512):
    t, d = x.shape
    ff = wg.shape[1]
    tm = min(tm, t)
    return pl.pallas_call(
        _ffn_body,
        grid=(t // tm, ff // tf),
        in_specs=[pl.BlockSpec((tm, d), lambda i, j: (i, 0)), pl.BlockSpec((d, tf), lambda i, j: (0, j)),
                  pl.BlockSpec((d, tf), lambda i, j: (0, j)), pl.BlockSpec((tf, d), lambda i, j: (j, 0)),
                  pl.BlockSpec((1, d), lambda i, j: (0, 0)), pl.BlockSpec((1, d), lambda i, j: (0, 0))],
        out_specs=pl.BlockSpec((tm, d), lambda i, j: (i, 0)),
        out_shape=jax.ShapeDtypeStruct((t, d), F32),
        scratch_shapes=[pltpu.VMEM((tm, d), F32)],
        compiler_params=_params("parallel", "arbitrary"),
        name="ffn",
    )(x, wg, wu, wd, g.reshape(1, d), b.reshape(1, d))


def _router_body(x_ref, w_ref, comb_ref, rank_ref, cnt_ref):
    logits = _dot_hi(x_ref[...], w_ref[...])
    lane = lax.broadcasted_iota(jnp.int32, logits.shape, 1)
    neg = jnp.float32(-jnp.inf)
    logits = jnp.where(lane < N_EXPERTS, logits, neg)
    m1 = jnp.max(logits, axis=-1, keepdims=True)
    i1 = jnp.min(jnp.where(logits == m1, lane, LANES), axis=-1, keepdims=True)
    rest = jnp.where(lane == i1, neg, logits)
    m2 = jnp.max(rest, axis=-1, keepdims=True)
    i2 = jnp.min(jnp.where(rest == m2, lane, LANES), axis=-1, keepdims=True)
    e2 = jnp.exp(m2 - m1)
    w1 = 1.0 / (1.0 + e2)
    w2 = e2 / (1.0 + e2)
    comb_ref[...] = jnp.where(lane == i1, w1, 0.0) + jnp.where(lane == i2, w2, 0.0)
    chosen = (lane == i1) | (lane == i2)
    tb = logits.shape[0]
    earlier = _tril(tb, strict=True)
    counts = _dot(jnp.where(earlier, 1.0, 0.0).astype(BF16), jnp.where(chosen, 1.0, 0.0).astype(BF16))
    rank_ref[...] = jnp.where(chosen, counts.astype(jnp.int32), -1)
    cnt_ref[0] = jnp.sum(jnp.where(chosen, 1.0, 0.0), axis=0, keepdims=True).astype(jnp.int32)


def _router(x, w_router, *, tb):
    t, d = x.shape
    w = jnp.pad(w_router, ((0, 0), (0, LANES - w_router.shape[1])))
    row = lambda i: (i, 0)
    return pl.pallas_call(
        _router_body,
        grid=(t // tb,),
        in_specs=[pl.BlockSpec((tb, d), row), pl.BlockSpec((d, LANES), lambda i: (0, 0))],
        out_specs=[pl.BlockSpec((tb, LANES), row), pl.BlockSpec((tb, LANES), row),
                   pl.BlockSpec((1, 1, LANES), lambda i: (i, 0, 0))],
        out_shape=[jax.ShapeDtypeStruct((t, LANES), F32), jax.ShapeDtypeStruct((t, LANES), jnp.int32),
                   jax.ShapeDtypeStruct((t // tb, 1, LANES), jnp.int32)],
        compiler_params=_params("parallel"),
        name="router",
    )(x, w)


MOE_ROW_TILE = 128


def _moe_body(cnt_ref, x_ref, comb_ref, rcol_ref, rrow_ref, wg_ref, wu_ref, wd_ref, g_ref, b_ref, o_ref,
              xb_s, xg_s, y_s, acc_s):
    blk, e, j = pl.program_id(0), pl.program_id(1), pl.program_id(2)
    tm = MOE_ROW_TILE
    tb = x_ref.shape[0]
    n_tiles = (cnt_ref[blk * pl.num_programs(1) + e] + (tm - 1)) // tm

    @pl.when((e == 0) & (j == 0))
    def _():
        xb_s[...] = x_ref[...].astype(BF16)
        acc_s[...] = jnp.zeros_like(acc_s)

    def tile_rows(t):
        return pl.ds(pl.multiple_of(t * tm, tm), tm)

    @pl.when(j == 0)
    def _():
        slot_of_token = rrow_ref[0, pl.ds(e, 1), :]
        row = lax.broadcasted_iota(jnp.int32, (tm, tb), 0)

        def gather(t, carry):
            onehot = jnp.where(slot_of_token - t * tm == row, 1.0, 0.0).astype(BF16)
            xg_s[tile_rows(t), :] = _dot(onehot, xb_s[...]).astype(BF16)
            y_s[tile_rows(t), :] = jnp.zeros((tm, y_s.shape[1]), F32)
            return carry

        lax.fori_loop(0, n_tiles, gather, 0)

    def expert(t, carry):
        xt = xg_s[tile_rows(t), :]
        a = _dot(xt, wg_ref[0])
        u = _dot(xt, wu_ref[0])
        y_s[tile_rows(t), :] += _dot((_silu(a) * u).astype(BF16), wd_ref[0])
        return carry

    lax.fori_loop(0, n_tiles, expert, 0)

    @pl.when(j == pl.num_programs(2) - 1)
    def _():
        lane = lax.broadcasted_iota(jnp.int32, (tb, LANES), 1)
        slot = jnp.sum(jnp.where(lane == e, rcol_ref[...], 0), axis=-1, keepdims=True)
        weight = jnp.sum(jnp.where(lane == e, comb_ref[...], 0.0), axis=-1, keepdims=True)
        col = lax.broadcasted_iota(jnp.int32, (tb, tm), 1)

        def scatter(t, carry):
            onehot = jnp.where(slot - t * tm == col, 1.0, 0.0).astype(BF16)
            acc_s[...] += weight * _dot(onehot, y_s[tile_rows(t), :].astype(BF16))
            return carry

        lax.fori_loop(0, n_tiles, scatter, 0)

    @pl.when((e == pl.num_programs(1) - 1) & (j == pl.num_programs(2) - 1))
    def _():
        o_ref[...] = _layer_norm(ALPHA * x_ref[...] + acc_s[...], g_ref[...], b_ref[...])


def _moe(x, comb, rank, counts, wg, wu, wd, g, b, *, tb, tf=896):
    t, d = x.shape
    ne, _, ff = wg.shape
    nb = t // tb
    rank_rows = rank.reshape(nb, tb, LANES)[:, :, :ne].transpose(0, 2, 1)
    tokens_per_expert = counts[:, 0, :ne].reshape(nb * ne)
    grid_spec = pltpu.PrefetchScalarGridSpec(
        num_scalar_prefetch=1,
        grid=(nb, ne, ff // tf),
        in_specs=[pl.BlockSpec((tb, d), lambda i, e, j, c: (i, 0)),
                  pl.BlockSpec((tb, LANES), lambda i, e, j, c: (i, 0)),
                  pl.BlockSpec((tb, LANES), lambda i, e, j, c: (i, 0)),
                  pl.BlockSpec((1, ne, tb), lambda i, e, j, c: (i, 0, 0)),
                  pl.BlockSpec((1, d, tf), lambda i, e, j, c: (e, 0, j)),
                  pl.BlockSpec((1, d, tf), lambda i, e, j, c: (e, 0, j)),
                  pl.BlockSpec((1, tf, d), lambda i, e, j, c: (e, j, 0)),
                  pl.BlockSpec((1, d), lambda i, e, j, c: (0, 0)), pl.BlockSpec((1, d), lambda i, e, j, c: (0, 0))],
        out_specs=pl.BlockSpec((tb, d), lambda i, e, j, c: (i, 0)),
        scratch_shapes=[pltpu.VMEM((tb, d), BF16), pltpu.VMEM((tb, d), BF16), pltpu.VMEM((tb, d), F32),
                        pltpu.VMEM((tb, d), F32)],
    )
    return pl.pallas_call(
        _moe_body,
        grid_spec=grid_spec,
        out_shape=jax.ShapeDtypeStruct((t, d), F32),
        compiler_params=_params("parallel", "arbitrary", "arbitrary"),
        name="moe",
    )(tokens_per_expert, x, comb, rank, rank_rows, wg, wu, wd, g.reshape(1, d), b.reshape(1, d))


def _moe_layer(h, w_router, wg, wu, wd, g, b, *, tb=1024):
    tb = min(tb, h.shape[0])
    comb, rank, counts = _router(h, w_router, tb=tb)
    return _moe(h, comb, rank, counts, wg.astype(BF16), wu.astype(BF16), wd.astype(BF16), g, b, tb=tb)


def _ple_body(h_ref, p_ref, wg_ref, bg_ref, wp_ref, o_ref):
    h = h_ref[...]
    gate = _sigmoid(_dot(h.astype(BF16), wg_ref[...]) + bg_ref[...])
    o_ref[...] = h + gate * _dot(p_ref[...].astype(BF16), wp_ref[...])


def _ple(h, p, wg, bg, wp, *, tm=1024):
    t, d = h.shape
    pd = p.shape[1]
    tm = min(tm, t)
    row = lambda i: (i, 0)
    const = lambda i: (0, 0)
    return pl.pallas_call(
        _ple_body,
        grid=(t // tm,),
        in_specs=[pl.BlockSpec((tm, d), row), pl.BlockSpec((tm, pd), row), pl.BlockSpec((d, d), const),
                  pl.BlockSpec((1, d), const), pl.BlockSpec((pd, d), const)],
        out_specs=pl.BlockSpec((tm, d), row),
        out_shape=jax.ShapeDtypeStruct((t, d), F32),
        compiler_params=_params("parallel"),
        name="ple",
    )(h, p, wg, bg.reshape(1, d), wp)


def _gmlp_gate_body(u_ref, v_ref, lng_ref, lnb_ref, ws_ref, bs_ref, wo_ref, h_ref, g_ref, b_ref, o_ref, gated_ref):
    tm, width = u_ref.shape
    gw = width // GM_GROUPS
    v = _layer_norm(v_ref[...].astype(F32), lng_ref[...], lnb_ref[...]).astype(BF16)
    mask = _tril(GM_CHUNK)
    for g in range(GM_GROUPS):
        ws = jnp.where(mask, ws_ref[g], 0.0).astype(BF16)
        bias = bs_ref[:, g:g + 1]
        for c in range(tm // GM_CHUNK):
            rows = slice(c * GM_CHUNK, (c + 1) * GM_CHUNK)
            cols = slice(g * gw, (g + 1) * gw)
            sv = _dot(ws, v[rows, cols]) + bias
            gated_ref[rows, cols] = (u_ref[rows, cols].astype(F32) * sv).astype(BF16)
    acc = _dot(gated_ref[...], wo_ref[...])
    o_ref[...] = _layer_norm(ALPHA * h_ref[...] + acc, g_ref[...], b_ref[...])


def _gmlp_mixer(h, w_in, b_in, ln_g, ln_b, w_s, b_s, w_out, g, b, *, tm=512):
    t, d = h.shape
    width = w_out.shape[0]
    tm = min(tm, t)
    z = _linear_gelu(h, w_in.astype(BF16), b_in)
    row = lambda i: (i, 0)
    const = lambda i: (0, 0)
    return pl.pallas_call(
        _gmlp_gate_body,
        grid=(t // tm,),
        in_specs=[pl.BlockSpec((tm, width), row), pl.BlockSpec((tm, width), lambda i: (i, 1)),
                  pl.BlockSpec((1, width), const), pl.BlockSpec((1, width), const),
                  pl.BlockSpec((GM_GROUPS, GM_CHUNK, GM_CHUNK), lambda i: (0, 0, 0)),
                  pl.BlockSpec((GM_CHUNK, GM_GROUPS), const), pl.BlockSpec((width, d), const),
                  pl.BlockSpec((tm, d), row), pl.BlockSpec((1, d), const), pl.BlockSpec((1, d), const)],
        out_specs=pl.BlockSpec((tm, d), row),
        out_shape=jax.ShapeDtypeStruct((t, d), F32),
        scratch_shapes=[pltpu.VMEM((tm, width), BF16)],
        compiler_params=_params("parallel"),
        name="gmlp_gate",
    )(z, z, ln_g.reshape(1, width), ln_b.reshape(1, width), w_s, b_s.T, w_out.astype(BF16), h,
      g.reshape(1, d), b.reshape(1, d))


def _fox_decay_body(f_ref, bf_ref, c_ref):
    s = f_ref.shape[1]
    x = f_ref[0] + bf_ref[...]
    log_f = jnp.minimum(x, 0.0) - jnp.log1p(jnp.exp(-jnp.abs(x)))
    tri = _tril(LANES).astype(F32)
    carry = jnp.zeros((1, LANES), F32)
    lane = lax.broadcasted_iota(jnp.int32, (LANES, LANES), 1)
    hds = FOX_HEADS
    for i in range(s // LANES):
        cs = _dot_hi(tri, log_f[i * LANES:(i + 1) * LANES]) + carry
        carry = cs[LANES - 1:LANES]
        hi = cs.astype(BF16).astype(F32)
        mid = (cs - hi).astype(BF16).astype(F32)
        lo = (cs - hi - mid).astype(BF16).astype(F32)
        out = jnp.where(lane < hds, hi, 0.0)
        for slot, piece in enumerate((mid, lo, -hi, -mid, -lo), start=1):
            moved = pltpu.roll(piece, slot * hds, 1)
            out = jnp.where((lane >= slot * hds) & (lane < (slot + 1) * hds), moved, out)
        c_ref[0, i * LANES:(i + 1) * LANES, :] = out.astype(c_ref.dtype)


def _fox_decay(f, b_f):
    bsz, s, _ = f.shape
    blk = lambda i: (i, 0, 0)
    return pl.pallas_call(
        _fox_decay_body,
        grid=(bsz,),
        in_specs=[pl.BlockSpec((1, s, LANES), blk), pl.BlockSpec((1, LANES), lambda i: (0, 0))],
        out_specs=pl.BlockSpec((1, s, LANES), blk),
        out_shape=jax.ShapeDtypeStruct((bsz, s, LANES), BF16),
        compiler_params=_params("parallel"),
        name="fox_decay",
    )(f, b_f)


FOX_BIAS_LANES = 6


def _fox_attn_body(q_ref, k_ref, v_ref, o_ref):
    qi = pl.program_id(2)
    tq = q_ref.shape[2]
    half = FOX_HEAD_DIM
    lane = lax.broadcasted_iota(jnp.int32, (1, LANES), 1)
    scale = jnp.where(lane < half, half ** -0.5, 1.0).astype(BF16)
    heads = range(q_ref.shape[1])
    qs = [q_ref[0, hh] * scale for hh in heads]
    diag = _tril(tq)

    def block(kj, carry, masked):
        rows = pl.ds(pl.multiple_of(kj * tq, tq), tq)
        ss = [_dot_nt(qs[hh], k_ref[0, hh, rows, :]) for hh in heads]
        if masked:
            ss = [jnp.where(diag, s, -1e30) for s in ss]
        ms = [jnp.maximum(carry[hh][0], jnp.max(ss[hh], axis=-1, keepdims=True)) for hh in heads]
        ps = [jnp.exp(ss[hh] - ms[hh]).astype(BF16) for hh in heads]
        pvs = [_dot(ps[hh], v_ref[0, hh, rows, :]) for hh in heads]
        return tuple((ms[hh], jnp.exp(carry[hh][0] - ms[hh]) * carry[hh][1] + pvs[hh]) for hh in heads)

    init = tuple((jnp.full((tq, 1), -1e30, F32), jnp.zeros((tq, LANES), F32)) for _ in heads)
    carry = lax.fori_loop(0, qi, functools.partial(block, masked=False), init)
    carry = block(qi, carry, masked=True)
    outs = [acc / acc[:, half:half + 1] for _, acc in carry]
    for pr in range(len(outs) // 2):
        pair = jnp.where(lane < half, outs[2 * pr], pltpu.roll(outs[2 * pr + 1], half, 1))
        o_ref[0, :, pr * LANES:(pr + 1) * LANES] = pair.astype(o_ref.dtype)


FOX_HEADS_PER_STEP = 4


def _fox_attention(q_aug, k_aug, v_aug, *, tq=256):
    bsz, hds, s, _ = q_aug.shape
    tq = min(tq, s)
    hps = FOX_HEADS_PER_STEP
    seq = pl.BlockSpec((1, hps, s, LANES), lambda b, h, i: (b, h, 0, 0))
    return pl.pallas_call(
        _fox_attn_body,
        grid=(bsz, hds // hps, s // tq),
        in_specs=[pl.BlockSpec((1, hps, tq, LANES), lambda b, h, i: (b, h, i, 0)), seq, seq],
        out_specs=pl.BlockSpec((1, tq, hps * FOX_HEAD_DIM), lambda b, h, i: (b, i, h)),
        out_shape=jax.ShapeDtypeStruct((bsz, s, hds * FOX_HEAD_DIM), BF16),
        compiler_params=_params("parallel", "parallel", "arbitrary"),
        name="fox_attn",
    )(q_aug, k_aug, v_aug)


def _fox_mixer(h, bsz, w_in, b_f, w_out, g, b):
    t, d = h.shape
    s = t // bsz
    hds, dh = FOX_HEADS, FOX_HEAD_DIM
    width = hds * dh
    qkv = _linear(h, w_in[:, :3 * width].astype(BF16))
    w_f = jnp.pad(w_in[:, 3 * width:], ((0, 0), (0, LANES - hds)))
    f = _linear(h, w_f, hi=True, out_dtype=F32, tn=LANES)
    b_pad = jnp.pad(b_f, (0, LANES - hds)).reshape(1, LANES)
    c6 = _fox_decay(f.reshape(bsz, s, LANES), b_pad)[:, :, :FOX_BIAS_LANES * hds]
    c6 = c6.reshape(bsz, s, FOX_BIAS_LANES, hds).transpose(0, 3, 1, 2)
    heads = lambda a: a.reshape(bsz, s, hds, dh).transpose(0, 2, 1, 3)
    qkv = qkv.reshape(bsz, s, 3 * width)
    ones3 = jnp.ones((bsz, hds, s, FOX_BIAS_LANES // 2), BF16)
    pad = jnp.zeros((bsz, hds, s, LANES - dh - FOX_BIAS_LANES), BF16)
    q_aug = jnp.concatenate([heads(qkv[:, :, :width]), c6[..., :3], ones3, pad], axis=-1)
    k_aug = jnp.concatenate([heads(qkv[:, :, width:2 * width]), ones3, c6[..., 3:], pad], axis=-1)
    v_aug = jnp.concatenate([heads(qkv[:, :, 2 * width:]), jnp.ones((bsz, hds, s, LANES - dh), BF16)], axis=-1)
    o = _fox_attention(q_aug, k_aug, v_aug)
    return _proj_ln(o.reshape(t, width), w_out.astype(BF16), h, g, b)


def _causal_conv(x, taps):
    k = len(taps)
    row = lax.broadcasted_iota(jnp.int32, x.shape, 0)
    out = x * taps[k - 1]
    for sh in range(1, k):
        shifted = jnp.where(row >= sh, pltpu.roll(x, sh, 0), 0.0)
        out = out + shifted * taps[k - 1 - sh]
    return out


def _gdn_gates_body(x_ref, dtb_ref, alog_ref, o_ref):
    x = x_ref[...]
    lane = lax.broadcasted_iota(jnp.int32, x.shape, 1)
    beta = _sigmoid(x)
    gdec = -jnp.exp(alog_ref[...]) * _softplus(x + dtb_ref[...])
    gdec = jnp.where((lane >= GDN_HEADS) & (lane < 2 * GDN_HEADS), gdec, 0.0)
    tri = _tril(GDN_CHUNK).astype(F32)
    lane_c = lax.broadcasted_iota(jnp.int32, (GDN_CHUNK, LANES), 1)
    for c in range(x.shape[0] // GDN_CHUNK):
        rows = slice(c * GDN_CHUNK, (c + 1) * GDN_CHUNK)
        gc = _dot_hi(tri, gdec[rows])
        o_ref[rows, :] = jnp.where(lane_c < GDN_HEADS, beta[rows], gc)


def _gdn_gates(x, dt_bias, a_log, *, tm=512):
    t = x.shape[0]
    tm = min(tm, t)
    pad = lambda v: jnp.pad(v, (GDN_HEADS, LANES - 2 * GDN_HEADS)).reshape(1, LANES)
    return pl.pallas_call(
        _gdn_gates_body,
        grid=(t // tm,),
        in_specs=[pl.BlockSpec((tm, LANES), lambda i: (i, 0)), pl.BlockSpec((1, LANES), lambda i: (0, 0)),
                  pl.BlockSpec((1, LANES), lambda i: (0, 0))],
        out_specs=pl.BlockSpec((tm, LANES), lambda i: (i, 0)),
        out_shape=jax.ShapeDtypeStruct((t, LANES), F32),
        compiler_params=_params("parallel"),
        name="gdn_gates",
    )(x, pad(dt_bias), pad(a_log))


def _unit_lower_inverses(ms):
    n = ms[0].shape[0]
    eye = (lax.broadcasted_iota(jnp.int32, (n, n), 0) == lax.broadcasted_iota(jnp.int32, (n, n), 1)).astype(F32)
    ps = [-m for m in ms]
    invs = [eye + p for p in ps]
    step = 1
    while 2 * step < n:
        ps = [_dot_split(p, p) for p in ps]
        invs = [inv + _dot_split(inv, p) for inv, p in zip(invs, ps)]
        step *= 2
    return invs


GDN_HEADS_PER_STEP = 2
GDN_CHUNKS_PER_ITER = 4


def _gdn_body(q_ref, k_ref, v_ref, gate_ref, cw_ref, gcol_ref, grow_ref, ng_ref, o_ref,
              q_s, k_s, kb_s, uw_s, g_s, a_s, o_s):
    hb = GDN_HEADS_PER_STEP
    grp = GDN_CHUNKS_PER_ITER
    hd0 = pl.program_id(1) * hb
    s = q_ref.shape[1]
    c = GDN_CHUNK
    dk = GDN_DK
    n_chunks = s // c

    gcols = gcol_ref[0]
    lane = lax.broadcasted_iota(jnp.int32, gcols.shape, 1)
    for hh in range(hb):
        cols = slice(hh * dk, (hh + 1) * dk)

        def conv_act(x_ref, which):
            taps = [cw_ref[hh, which, i:i + 1, :] for i in range(cw_ref.shape[2])]
            return _silu(_causal_conv(x_ref[0, :, cols].astype(F32), taps))

        q = conv_act(q_ref, 0)
        k = conv_act(k_ref, 1)
        v = conv_act(v_ref, 2)
        q = q * lax.rsqrt(jnp.sum(q * q, axis=-1, keepdims=True) + L2_EPS) * (dk ** -0.5)
        k = k * lax.rsqrt(jnp.sum(k * k, axis=-1, keepdims=True) + L2_EPS)
        beta = jnp.sum(jnp.where(lane == hd0 + hh, gcols, 0.0), axis=-1, keepdims=True)
        gc = jnp.sum(jnp.where(lane == hd0 + hh + GDN_HEADS, gcols, 0.0), axis=-1, keepdims=True)
        kb = k * beta
        q_s[hh] = q
        k_s[hh] = k
        kb_s[hh] = kb
        uw_s[hh, :, 0:dk] = v * beta
        uw_s[hh, :, dk:2 * dk] = kb * jnp.exp(gc)
        g_s[hh] = jnp.broadcast_to(gc, g_s.shape[1:])
    causal = _tril(c)
    strict = _tril(c, strict=True)

    def prep(it, carry):
        base = pl.multiple_of(it * (grp * c), grp * c)
        where, ms, aqks = [], [], []
        for hh in range(hb):
            for gi in range(grp):
                rows = pl.ds(base + gi * c, c)
                kc = k_s[hh, rows, :]
                gcc = g_s[hh, rows, 0:1]
                grow = grow_ref[0, hh, pl.ds(it * grp + gi, 1), :]
                decay = jnp.exp(jnp.where(causal, gcc - grow, -jnp.inf))
                ms.append(jnp.where(strict, _dot_nt(kb_s[hh, rows, :], kc) * decay, 0.0))
                aqks.append(_dot_nt(q_s[hh, rows, :], kc) * decay)
                where.append((hh, gi, rows))
        invs = _unit_lower_inverses(ms)
        uws = [_dot_split(inv, uw_s[hh, rows, :]) for inv, (hh, gi, rows) in zip(invs, where)]
        for (hh, gi, rows), aqk, uw in zip(where, aqks, uws):
            a_s[hh, it * grp + gi] = aqk
            uw_s[hh, rows, :] = uw
        return carry

    lax.fori_loop(0, n_chunks // grp, prep, 0)

    def scan(i, states):
        rows = pl.ds(pl.multiple_of(i * c, c), c)
        new_states = []
        for hh in range(hb):
            state = states[hh]
            gcc = g_s[hh, rows, 0:1]
            g_last = g_s[hh, pl.ds(i * c + c - 1, 1), 0:1]
            v_new = uw_s[hh, rows, 0:dk] - _dot(uw_s[hh, rows, dk:2 * dk], state)
            o_s[hh, rows, :] = _dot(q_s[hh, rows, :] * jnp.exp(gcc), state) + _dot(a_s[hh, i], v_new)
            kd = k_s[hh, rows, :] * jnp.exp(g_last - gcc)
            new_states.append(state * jnp.exp(g_last) + _dot_tn(kd, v_new))
        return tuple(new_states)

    lax.fori_loop(0, n_chunks, scan, tuple(jnp.zeros((dk, dk), F32) for _ in range(hb)))
    for hh in range(hb):
        cols = slice(hh * dk, (hh + 1) * dk)
        o = o_s[hh]
        o = o * lax.rsqrt(jnp.mean(o * o, axis=-1, keepdims=True) + RMS_EPS) * ng_ref[...]
        o_ref[0, :, cols] = (o * _silu(gate_ref[0, :, cols].astype(F32))).astype(o_ref.dtype)


def _gdn_core(proj, conv_w, gcol, grow, norm_g):
    bsz, s, _ = proj.shape
    hd, dk, c, hb = GDN_HEADS, GDN_DK, GDN_CHUNK, GDN_HEADS_PER_STEP
    n = s // c
    steps = hd // hb
    cw = conv_w.reshape(conv_w.shape[0], 3, hd, dk).transpose(2, 1, 0, 3)
    kk = conv_w.shape[0]
    seq = lambda off: pl.BlockSpec((1, s, hb * dk), lambda b, h: (b, 0, off + h))
    per_head = lambda *shape: pltpu.VMEM((hb,) + shape, F32)
    return pl.pallas_call(
        _gdn_body,
        grid=(bsz, steps),
        in_specs=[seq(0), seq(steps), seq(2 * steps), seq(3 * steps),
                  pl.BlockSpec((hb, 3, kk, dk), lambda b, h: (h, 0, 0, 0)),
                  pl.BlockSpec((1, s, LANES), lambda b, h: (b, 0, 0)),
                  pl.BlockSpec((1, hb, n, c), lambda b, h: (b, h, 0, 0)),
                  pl.BlockSpec((1, dk), lambda b, h: (0, 0))],
        out_specs=seq(0),
        out_shape=jax.ShapeDtypeStruct((bsz, s, hd * dk), BF16),
        scratch_shapes=[per_head(s, dk), per_head(s, dk), per_head(s, dk), per_head(s, 2 * dk), per_head(s, LANES),
                        per_head(n, c, c), per_head(s, dk)],
        compiler_params=_params("parallel", "arbitrary"),
        name="gdn_core",
    )(proj, proj, proj, proj, cw, gcol, grow, norm_g.reshape(1, dk))


def _gdn_mixer(h, bsz, w_in, conv_w, a_log, dt_bias, norm_g, w_out, g, b):
    t, d = h.shape
    s = t // bsz
    hd = GDN_HEADS
    qkv_w = 3 * hd * GDN_DK
    gate_w = hd * GDN_DK
    proj = _linear(h, w_in[:, :qkv_w + gate_w].astype(BF16))
    w_small = jnp.pad(w_in[:, qkv_w + gate_w:], ((0, 0), (0, LANES - 2 * hd)))
    logits = _linear(h, w_small, hi=True, out_dtype=F32, tn=LANES)
    gcol = _gdn_gates(logits, dt_bias, a_log).reshape(bsz, s, LANES)
    grow = gcol[:, :, hd:2 * hd].transpose(0, 2, 1).reshape(bsz, hd, s // GDN_CHUNK, GDN_CHUNK)
    o = _gdn_core(proj.reshape(bsz, s, qkv_w + gate_w), conv_w, gcol, grow, norm_g)
    return _proj_ln(o.reshape(t, gate_w), w_out.astype(BF16), h, g, b)


def _ssd_gates_body(x_ref, dtb_ref, alog_ref, o_ref):
    x = x_ref[...]
    lane = lax.broadcasted_iota(jnp.int32, x.shape, 1)
    dt = _softplus(x + dtb_ref[...])
    a = jnp.where(lane < SSD_HEADS, dt * (-jnp.exp(alog_ref[...])), 0.0)
    tri = _tril(SSD_CHUNK).astype(F32)
    lane_c = lax.broadcasted_iota(jnp.int32, (SSD_CHUNK, LANES), 1)
    for c in range(x.shape[0] // SSD_CHUNK):
        rows = slice(c * SSD_CHUNK, (c + 1) * SSD_CHUNK)
        acum = pltpu.roll(_dot_hi(tri, a[rows]), SSD_HEADS, 1)
        o_ref[rows, :] = jnp.where(lane_c < SSD_HEADS, dt[rows], acum)


def _ssd_gates(x, dt_bias, a_log, *, tm=512):
    t = x.shape[0]
    tm = min(tm, t)
    pad = lambda v: jnp.pad(v, (0, LANES - SSD_HEADS)).reshape(1, LANES)
    return pl.pallas_call(
        _ssd_gates_body,
        grid=(t // tm,),
        in_specs=[pl.BlockSpec((tm, LANES), lambda i: (i, 0)), pl.BlockSpec((1, LANES), lambda i: (0, 0)),
                  pl.BlockSpec((1, LANES), lambda i: (0, 0))],
        out_specs=pl.BlockSpec((tm, LANES), lambda i: (i, 0)),
        out_shape=jax.ShapeDtypeStruct((t, LANES), F32),
        compiler_params=_params("parallel"),
        name="ssd_gates",
    )(x, pad(dt_bias), pad(a_log))


def _ssd_body(x_ref, b_ref, c_ref, z_ref, cwx_ref, cwb_ref, cwc_ref, cbx_ref, cbb_ref, cbc_ref,
              gcol_ref, arow_ref, d_ref, ng_ref, o_ref, x_s, b_s, c_s, y_s):
    grp = pl.program_id(1)
    s = x_ref.shape[1]
    c = SSD_CHUNK
    hp = SSD_HEAD_DIM
    heads = x_ref.shape[2] // hp
    n_chunks = s // c

    def conv_act(ref, w_ref, bias_ref):
        taps = [w_ref[i:i + 1, :] for i in range(w_ref.shape[0])]
        return _silu(_causal_conv(ref[0].astype(F32), taps) + bias_ref[...])

    x_s[...] = conv_act(x_ref, cwx_ref, cbx_ref)
    b_s[...] = conv_act(b_ref, cwb_ref, cbb_ref)
    c_s[...] = conv_act(c_ref, cwc_ref, cbc_ref)
    causal = _tril(c)
    pair_lane = lax.broadcasted_iota(jnp.int32, (c, 2 * hp), 1) < hp

    def head_cols(gcol, base):
        lane = lax.broadcasted_iota(jnp.int32, gcol.shape, 1)
        return [jnp.sum(jnp.where(lane == base + grp * heads + j, gcol, 0.0), axis=-1, keepdims=True)
                for j in range(heads)]

    def widen(cols):
        pairs = [jnp.where(pair_lane, cols[2 * p], cols[2 * p + 1]) for p in range(heads // 2)]
        return jnp.concatenate(pairs, axis=1)

    def step(i, state):
        rows = pl.ds(pl.multiple_of(i * c, c), c)
        gcol = gcol_ref[0, rows, :]
        dt = head_cols(gcol, 0)
        acum = head_cols(gcol, SSD_HEADS)
        arow = arow_ref[0, 0, i]
        xc = x_s[rows, :]
        bc = b_s[rows, :]
        cc = c_s[rows, :]
        xdt = xc * widen(dt)
        cb = _dot_nt(cc, bc)
        y = _dot(cc, state) * widen([jnp.exp(a) for a in acum])
        intra = []
        for p in range(heads // 2):
            xp = xdt[:, 2 * p * hp:(2 * p + 2) * hp]
            ys = []
            for j in (2 * p, 2 * p + 1):
                lmat = jnp.exp(jnp.where(causal, acum[j] - arow[j:j + 1, :], -jnp.inf))
                ys.append(_dot(cb * lmat, xp))
            intra.append(jnp.where(pair_lane, ys[0], ys[1]))
        y = y + jnp.concatenate(intra, axis=1) + xc * d_ref[...]
        y_s[rows, :] = y
        last = head_cols(gcol_ref[0, pl.ds(i * c + c - 1, 1), :], SSD_HEADS)
        xdec = xdt * widen([jnp.exp(last[j] - acum[j]) for j in range(heads)])
        first_lane = lax.broadcasted_iota(jnp.int32, (1, 2 * hp), 1) < hp
        lastw = jnp.concatenate([jnp.where(first_lane, jnp.exp(last[2 * p]), jnp.exp(last[2 * p + 1]))
                                 for p in range(heads // 2)], axis=1)
        return state * lastw + _dot_tn(bc, xdec)

    lax.fori_loop(0, n_chunks, step, jnp.zeros((SSD_STATE, x_ref.shape[2]), F32))
    y = y_s[...] * _silu(z_ref[0].astype(F32))
    y = y * lax.rsqrt(jnp.mean(y * y, axis=-1, keepdims=True) + RMS_EPS) * ng_ref[...]
    o_ref[0] = y.astype(o_ref.dtype)


def _ssd_core(proj, conv_w, conv_b, gcol, arow, d_wide, norm_g):
    bsz, s, _ = proj.shape
    grp, n_state, c = SSD_GROUPS, SSD_STATE, SSD_CHUNK
    inner = SSD_HEADS * SSD_HEAD_DIM
    gw = inner // grp
    heads = SSD_HEADS // grp
    n = s // c
    kk = conv_w.shape[0]
    xblk = lambda off: pl.BlockSpec((1, s, gw), lambda b, g: (b, 0, off + g))
    nblk = lambda off: pl.BlockSpec((1, s, n_state), lambda b, g: (b, 0, off + g))
    wx = lambda rows, off: pl.BlockSpec((rows, gw), lambda b, g: (0, off + g))
    wn = lambda rows, off: pl.BlockSpec((rows, n_state), lambda b, g: (0, off + g))
    x_off = inner // gw
    b_off = 2 * inner // n_state
    c_off = b_off + grp
    return pl.pallas_call(
        _ssd_body,
        grid=(bsz, grp),
        in_specs=[xblk(x_off), nblk(b_off), nblk(c_off), xblk(0),
                  wx(kk, 0), wn(kk, inner // n_state), wn(kk, inner // n_state + grp),
                  wx(1, 0), wn(1, inner // n_state), wn(1, inner // n_state + grp),
                  pl.BlockSpec((1, s, LANES), lambda b, g: (b, 0, 0)),
                  pl.BlockSpec((1, 1, n, heads, c), lambda b, g: (b, g, 0, 0, 0)),
                  wx(1, 0), wx(1, 0)],
        out_specs=xblk(0),
        out_shape=jax.ShapeDtypeStruct((bsz, s, inner), BF16),
        scratch_shapes=[pltpu.VMEM((s, gw), F32), pltpu.VMEM((s, n_state), F32), pltpu.VMEM((s, n_state), F32),
                        pltpu.VMEM((s, gw), F32)],
        compiler_params=_params("parallel", "arbitrary"),
        name="ssd_core",
    )(proj, proj, proj, proj, conv_w, conv_w, conv_w, conv_b, conv_b, conv_b, gcol, arow, d_wide, norm_g)


def _ssd_mixer(h, bsz, w_in, conv_w, conv_b, dt_bias, a_log, d_skip, norm_g, w_out, g, b):
    t, d = h.shape
    s = t // bsz
    inner = SSD_HEADS * SSD_HEAD_DIM
    main = 2 * inner + 2 * SSD_GROUPS * SSD_STATE
    heads = SSD_HEADS // SSD_GROUPS
    proj = _linear(h, w_in[:, :main].astype(BF16))
    w_dt = jnp.pad(w_in[:, main:], ((0, 0), (0, LANES - SSD_HEADS)))
    dt_logits = _linear(h, w_dt, hi=True, out_dtype=F32, tn=LANES)
    gcol = _ssd_gates(dt_logits, dt_bias, a_log).reshape(bsz, s, LANES)
    arow = gcol[:, :, SSD_HEADS:2 * SSD_HEADS].reshape(bsz, s // SSD_CHUNK, SSD_CHUNK, SSD_GROUPS, heads)
    arow = arow.transpose(0, 3, 1, 4, 2)
    d_wide = jnp.repeat(d_skip, SSD_HEAD_DIM).reshape(1, inner)
    o = _ssd_core(proj.reshape(bsz, s, main), conv_w, conv_b.reshape(1, -1), gcol, arow, d_wide,
                  norm_g.reshape(1, inner))
    return _proj_ln(o.reshape(t, inner), w_out.astype(BF16), h, g, b)


def kernel(x, p, ln_g, ln_b, gm_w_in, gm_b_in, gm_ln_g, gm_ln_b, gm_w_s, gm_b_s, gm_w_out, fox_w_in, fox_b_f, fox_w_out, gdn_w_in, gdn_conv_w, gdn_a_log, gdn_dt_bias, gdn_norm_g, gdn_w_out, ssd_w_in, ssd_conv_w, ssd_conv_b, ssd_dt_bias, ssd_a_log, ssd_d, ssd_norm_g, ssd_w_out, ffn_w_gate, ffn_w_up, ffn_w_down, moe_w_router, moe_w_gate, moe_w_up, moe_w_down, ple_w_gate, ple_b_gate, ple_w_proj):
    bsz, s, d = x.shape
    t = bsz * s
    h = x.reshape(t, d)
    pf = p.reshape(p.shape[0], t, p.shape[-1])
    for i in range(DEPTH):
        m, j = i % 4, i // 4
        g0, b0 = ln_g[i, 0], ln_b[i, 0]
        if m == 0:
            h = _gmlp_mixer(h, gm_w_in[j], gm_b_in[j], gm_ln_g[j], gm_ln_b[j], gm_w_s[j], gm_b_s[j], gm_w_out[j], g0, b0)
        elif m == 1:
            h = _fox_mixer(h, bsz, fox_w_in[j], fox_b_f[j], fox_w_out[j], g0, b0)
        elif m == 2:
            h = _gdn_mixer(h, bsz, gdn_w_in[j], gdn_conv_w[j], gdn_a_log[j], gdn_dt_bias[j], gdn_norm_g[j], gdn_w_out[j],
                           g0, b0)
        else:
            h = _ssd_mixer(h, bsz, ssd_w_in[j], ssd_conv_w[j], ssd_conv_b[j], ssd_dt_bias[j], ssd_a_log[j], ssd_d[j],
                           ssd_norm_g[j], ssd_w_out[j], g0, b0)
        g1, b1 = ln_g[i, 1], ln_b[i, 1]
        if i % 2 == 0:
            h = _ffn(h, ffn_w_gate[i // 2].astype(BF16), ffn_w_up[i // 2].astype(BF16), ffn_w_down[i // 2].astype(BF16),
                     g1, b1)
        else:
            h = _moe_layer(h, moe_w_router[i // 2], moe_w_gate[i // 2], moe_w_up[i // 2], moe_w_down[i // 2], g1, b1)
        h = _ple(h, pf[i], ple_w_gate[i].astype(BF16), ple_b_gate[i], ple_w_proj[i].astype(BF16))
    return h.reshape(bsz, s, d)
```

```python
import functools
import math

import jax
import jax.numpy as jnp
from jax import lax
from jax.experimental import pallas as pl
from jax.experimental.pallas import tpu as pltpu

F32 = jnp.float32
BF16 = jnp.bfloat16
HIGHEST = lax.Precision.HIGHEST

DEPTH = 4
ALPHA = (2 * DEPTH) ** 0.25
LN_EPS = 1e-5
RMS_EPS = 1e-6
L2_EPS = 1e-6

GM_CHUNK = 128
GM_GROUPS = 8
FOX_HEADS = 16
FOX_HEAD_DIM = 64
GDN_HEADS = 8
GDN_DK = 128
GDN_CHUNK = 64
SSD_HEADS = 32
SSD_HEAD_DIM = 64
SSD_GROUPS = 4
SSD_STATE = 128
SSD_CHUNK = 64
N_EXPERTS = 8
LANES = 128

VMEM_LIMIT_BYTES = 56 * 1024 * 1024
MOE_VMEM_LIMIT_BYTES = 60 * 1024 * 1024


def _params(*sem):
    return pltpu.CompilerParams(dimension_semantics=sem, vmem_limit_bytes=VMEM_LIMIT_BYTES)


def _dot(a, b):
    return jnp.dot(a, b, preferred_element_type=F32)


def _dot_nt(a, b):
    return lax.dot_general(a, b, (((1,), (1,)), ((), ())), preferred_element_type=F32)


def _dot_tn(a, b):
    return lax.dot_general(a, b, (((0,), (0,)), ((), ())), preferred_element_type=F32)


def _dot_hi(a, b):
    return jnp.dot(a, b, preferred_element_type=F32, precision=HIGHEST)


def _split_bf16(a):
    hi = a.astype(BF16)
    return hi, (a - hi.astype(F32)).astype(BF16)


def _dot_split(a, b):
    ah, al = _split_bf16(a)
    bh, bl = _split_bf16(b)
    return _dot(ah, bh) + (_dot(ah, bl) + _dot(al, bh))


def _layer_norm(r, g, b):
    mu = jnp.mean(r, axis=-1, keepdims=True)
    d = r - mu
    var = jnp.mean(d * d, axis=-1, keepdims=True)
    return d * lax.rsqrt(var + LN_EPS) * g + b


def _sigmoid(x):
    return 1.0 / (1.0 + jnp.exp(-x))


def _silu(x):
    return x * _sigmoid(x)


def _softplus(x):
    return jnp.maximum(x, 0.0) + jnp.log1p(jnp.exp(-jnp.abs(x)))


def _tril(n, strict=False):
    r = lax.broadcasted_iota(jnp.int32, (n, n), 0)
    c = lax.broadcasted_iota(jnp.int32, (n, n), 1)
    return (r > c) if strict else (r >= c)


def _linear_body(x_ref, w_ref, o_ref, *, hi):
    if hi:
        y = _dot_split(x_ref[...], w_ref[...])
    else:
        y = _dot(x_ref[...].astype(BF16), w_ref[...])
    o_ref[...] = y.astype(o_ref.dtype)


def _linear(x, w, *, hi=False, out_dtype=BF16, tm=1024, tn=1024):
    t, k = x.shape
    n = w.shape[1]
    tm, tn = min(tm, t), min(tn, n)
    return pl.pallas_call(
        functools.partial(_linear_body, hi=hi),
        grid=(t // tm, n // tn),
        in_specs=[pl.BlockSpec((tm, k), lambda i, j: (i, 0)), pl.BlockSpec((k, tn), lambda i, j: (0, j))],
        out_specs=pl.BlockSpec((tm, tn), lambda i, j: (i, j)),
        out_shape=jax.ShapeDtypeStruct((t, n), out_dtype),
        compiler_params=_params("parallel", "arbitrary"),
        name="linear_hi" if hi else "linear",
    )(x, w)


def _linear_gelu_body(x_ref, w_ref, b_ref, o_ref):
    y = _dot(x_ref[...].astype(BF16), w_ref[...]) + b_ref[...]
    o_ref[...] = jax.nn.gelu(y).astype(o_ref.dtype)


def _linear_gelu(x, w, b, *, tm=1024, tn=1024):
    t, k = x.shape
    n = w.shape[1]
    tm = min(tm, t)
    return pl.pallas_call(
        _linear_gelu_body,
        grid=(t // tm, n // tn),
        in_specs=[pl.BlockSpec((tm, k), lambda i, j: (i, 0)), pl.BlockSpec((k, tn), lambda i, j: (0, j)),
                  pl.BlockSpec((1, tn), lambda i, j: (0, j))],
        out_specs=pl.BlockSpec((tm, tn), lambda i, j: (i, j)),
        out_shape=jax.ShapeDtypeStruct((t, n), BF16),
        compiler_params=_params("parallel", "arbitrary"),
        name="linear_gelu",
    )(x, w, b.reshape(1, n))


def _proj_ln_body(y_ref, w_ref, h_ref, g_ref, b_ref, o_ref):
    acc = _dot(y_ref[...].astype(BF16), w_ref[...])
    o_ref[...] = _layer_norm(ALPHA * h_ref[...] + acc, g_ref[...], b_ref[...])


def _proj_ln(y, w, h, g, b, *, tm=512):
    t, k = y.shape
    d = w.shape[1]
    tm = min(tm, t)
    row = lambda i: (i, 0)
    const = lambda i: (0, 0)
    return pl.pallas_call(
        _proj_ln_body,
        grid=(t // tm,),
        in_specs=[pl.BlockSpec((tm, k), row), pl.BlockSpec((k, d), const), pl.BlockSpec((tm, d), row),
                  pl.BlockSpec((1, d), const), pl.BlockSpec((1, d), const)],
        out_specs=pl.BlockSpec((tm, d), row),
        out_shape=jax.ShapeDtypeStruct((t, d), F32),
        compiler_params=_params("parallel"),
        name="proj_ln",
    )(y, w, h, g.reshape(1, d), b.reshape(1, d))


def _norm_then_ple(r, g_ref, b_ref, p_ref, pwg_ref, pbg_ref, pwp_ref):
    h = _layer_norm(r, g_ref[...], b_ref[...])
    gate = _sigmoid(_dot(h.astype(BF16), pwg_ref[...]) + pbg_ref[...])
    return h + gate * _dot(p_ref[...].astype(BF16), pwp_ref[...])


def _ffn_body(x_ref, wg_ref, wu_ref, wd_ref, g_ref, b_ref, p_ref, pwg_ref, pbg_ref, pwp_ref, o_ref, acc_ref):
    j = pl.program_id(1)

    @pl.when(j == 0)
    def _():
        acc_ref[...] = jnp.zeros_like(acc_ref)

    x = x_ref[...].astype(BF16)
    a = _dot(x, wg_ref[...])
    u = _dot(x, wu_ref[...])
    acc_ref[...] += _dot((_silu(a) * u).astype(BF16), wd_ref[...])

    @pl.when(j == pl.num_programs(1) - 1)
    def _():
        o_ref[...] = _norm_then_ple(ALPHA * x_ref[...] + acc_ref[...], g_ref, b_ref, p_ref, pwg_ref, pbg_ref, pwp_ref)


def _ffn(x, wg, wu, wd, g, b, p, ple_wg, ple_bg, ple_wp, *, tm=1024, tf=512):
    t, d = x.shape
    ff = wg.shape[1]
    pd = p.shape[1]
    tm = min(tm, t)
    const = lambda i, j: (0, 0)
    return pl.pallas_call(
        _ffn_body,
        grid=(t // tm, ff // tf),
        in_specs=[pl.BlockSpec((tm, d), lambda i, j: (i, 0)), pl.BlockSpec((d, tf), lambda i, j: (0, j)),
                  pl.BlockSpec((d, tf), lambda i, j: (0, j)), pl.BlockSpec((tf, d), lambda i, j: (j, 0)),
                  pl.BlockSpec((1, d), const), pl.BlockSpec((1, d), const),
                  pl.BlockSpec((tm, pd), lambda i, j: (i, 0)), pl.BlockSpec((d, d), const), pl.BlockSpec((1, d), const),
                  pl.BlockSpec((pd, d), const)],
        out_specs=pl.BlockSpec((tm, d), lambda i, j: (i, 0)),
        out_shape=jax.ShapeDtypeStruct((t, d), F32),
        scratch_shapes=[pltpu.VMEM((tm, d), F32)],
        compiler_params=_params("parallel", "arbitrary"),
        name="ffn",
    )(x, wg, wu, wd, g.reshape(1, d), b.reshape(1, d), p, ple_wg, ple_bg.reshape(1, d), ple_wp)


def _router_body(x_ref, w_ref, comb_ref, rank_ref, start_ref):
    logits = _dot_split(x_ref[...], w_ref[...])
    lane = lax.broadcasted_iota(jnp.int32, logits.shape, 1)
    neg = jnp.float32(-jnp.inf)
    logits = jnp.where(lane < N_EXPERTS, logits, neg)
    m1 = jnp.max(logits, axis=-1, keepdims=True)
    i1 = jnp.min(jnp.where(logits == m1, lane, LANES), axis=-1, keepdims=True)
    rest = jnp.where(lane == i1, neg, logits)
    m2 = jnp.max(rest, axis=-1, keepdims=True)
    i2 = jnp.min(jnp.where(rest == m2, lane, LANES), axis=-1, keepdims=True)
    e2 = jnp.exp(m2 - m1)
    w1 = 1.0 / (1.0 + e2)
    w2 = e2 / (1.0 + e2)
    comb_ref[...] = jnp.where(lane == i1, w1, 0.0) + jnp.where(lane == i2, w2, 0.0)
    chosen = jnp.where((lane == i1) | (lane == i2), 1.0, 0.0)
    tb = logits.shape[0]
    sub = MOE_SUB
    earlier = jnp.where(_tril(sub, strict=True), 1.0, 0.0).astype(BF16)
    before = jnp.zeros((1, LANES), F32)
    for s in range(tb // sub):
        rows = slice(s * sub, (s + 1) * sub)
        picked = chosen[rows]
        counts = _dot(earlier, picked.astype(BF16)) + before
        rank_ref[rows, :] = jnp.where(picked > 0.0, counts.astype(jnp.int32), -1)
        start_ref[0, s:s + 1, :] = before.astype(jnp.int32)
        before = before + jnp.sum(picked, axis=0, keepdims=True)
    start_ref[0, tb // sub:tb // sub + 1, :] = before.astype(jnp.int32)


def _router(x, w_router, *, tb):
    t, d = x.shape
    w = jnp.pad(w_router, ((0, 0), (0, LANES - w_router.shape[1])))
    row = lambda i: (i, 0)
    marks = tb // MOE_SUB + 1
    return pl.pallas_call(
        _router_body,
        grid=(t // tb,),
        in_specs=[pl.BlockSpec((tb, d), row), pl.BlockSpec((d, LANES), lambda i: (0, 0))],
        out_specs=[pl.BlockSpec((tb, LANES), row), pl.BlockSpec((tb, LANES), row),
                   pl.BlockSpec((1, marks, LANES), lambda i: (i, 0, 0))],
        out_shape=[jax.ShapeDtypeStruct((t, LANES), F32), jax.ShapeDtypeStruct((t, LANES), jnp.int32),
                   jax.ShapeDtypeStruct((t // tb, marks, LANES), jnp.int32)],
        compiler_params=_params("parallel"),
        name="router",
    )(x, w)


MOE_ROW_TILE = 128
MOE_SUB = 256


def _moe_body(start_ref, x_ref, comb_ref, rcol_ref, rrow_ref, wg_ref, wu_ref, wd_ref, g_ref, b_ref,
              p_ref, pwg_ref, pbg_ref, pwp_ref, o_ref, xb_s, xg_s, y_s):
    blk, e, j = pl.program_id(0), pl.program_id(1), pl.program_id(2)
    tm, sub = MOE_ROW_TILE, MOE_SUB
    tb, d = x_ref.shape
    n_sub = tb // sub
    base = (blk * pl.num_programs(1) + e) * (n_sub + 1)
    starts = [start_ref[base + s] for s in range(n_sub + 1)]
    n_tiles = (starts[n_sub] + (tm - 1)) // tm

    @pl.when((e == 0) & (j == 0))
    def _():
        xb_s[...] = x_ref[...].astype(BF16)
        o_ref[...] = jnp.zeros_like(o_ref)

    def tile_rows(t):
        return pl.ds(pl.multiple_of(t * tm, tm), tm)

    def overlaps(s, lo):
        return (starts[s] < lo + tm) & (starts[s + 1] > lo)

    @pl.when(j == 0)
    def _():
        row = lax.broadcasted_iota(jnp.int32, (tm, sub), 0)

        def gather(t, carry):
            lo = t * tm
            y_s[tile_rows(t), :] = jnp.zeros((tm, d), F32)
            for s in range(n_sub):
                @pl.when(overlaps(s, lo))
                def _():
                    slot_of_token = rrow_ref[0, pl.ds(e, 1), s * sub:(s + 1) * sub]
                    onehot = jnp.where(slot_of_token - lo == row, 1.0, 0.0).astype(BF16)
                    y_s[tile_rows(t), :] += _dot(onehot, xb_s[s * sub:(s + 1) * sub, :])
            xg_s[tile_rows(t), :] = y_s[tile_rows(t), :].astype(BF16)
            y_s[tile_rows(t), :] = jnp.zeros((tm, d), F32)
            return carry

        lax.fori_loop(0, n_tiles, gather, 0)

    def expert(t, carry):
        xt = xg_s[tile_rows(t), :]
        a = _dot(xt, wg_ref[0])
        u = _dot(xt, wu_ref[0])
        y_s[tile_rows(t), :] += _dot((_silu(a) * u).astype(BF16), wd_ref[0])
        return carry

    lax.fori_loop(0, n_tiles, expert, 0)

    @pl.when(j == pl.num_programs(2) - 1)
    def _():
        lane = lax.broadcasted_iota(jnp.int32, (sub, LANES), 1)
        col = lax.broadcasted_iota(jnp.int32, (sub, tm), 1)

        def scatter(t, carry):
            lo = t * tm
            yt = y_s[tile_rows(t), :].astype(BF16)
            for s in range(n_sub):
                @pl.when(overlaps(s, lo))
                def _():
                    rs = slice(s * sub, (s + 1) * sub)
                    slot = jnp.sum(jnp.where(lane == e, rcol_ref[rs, :], 0), axis=-1, keepdims=True)
                    weight = jnp.sum(jnp.where(lane == e, comb_ref[rs, :], 0.0), axis=-1, keepdims=True)
                    onehot = jnp.where(slot - lo == col, 1.0, 0.0).astype(BF16)
                    o_ref[rs, :] += weight * _dot(onehot, yt)
            return carry

        lax.fori_loop(0, n_tiles, scatter, 0)

    @pl.when((e == pl.num_programs(1) - 1) & (j == pl.num_programs(2) - 1))
    def _():
        def finish(r, carry):
            rs = pl.ds(pl.multiple_of(r * sub, sub), sub)
            o_ref[rs, :] = _norm_then_ple(ALPHA * x_ref[rs, :] + o_ref[rs, :], g_ref, b_ref, p_ref.at[rs, :],
                                          pwg_ref, pbg_ref, pwp_ref)
            return carry

        lax.fori_loop(0, n_sub, finish, 0)


def _moe(x, comb, rank, starts, wg, wu, wd, g, b, p, ple_wg, ple_bg, ple_wp, *, tb, tf=896):
    t, d = x.shape
    ne, _, ff = wg.shape
    pd = p.shape[1]
    nb = t // tb
    rank_rows = rank.reshape(nb, tb, LANES)[:, :, :ne].transpose(0, 2, 1)
    starts = starts[:, :, :ne].transpose(0, 2, 1).reshape(-1)
    once = pl.Buffered(1)
    grid_spec = pltpu.PrefetchScalarGridSpec(
        num_scalar_prefetch=1,
        grid=(nb, ne, ff // tf),
        in_specs=[pl.BlockSpec((tb, d), lambda i, e, j, c: (i, 0), pipeline_mode=once),
                  pl.BlockSpec((tb, LANES), lambda i, e, j, c: (i, 0), pipeline_mode=once),
                  pl.BlockSpec((tb, LANES), lambda i, e, j, c: (i, 0), pipeline_mode=once),
                  pl.BlockSpec((1, ne, tb), lambda i, e, j, c: (i, 0, 0), pipeline_mode=once),
                  pl.BlockSpec((1, d, tf), lambda i, e, j, c: (e, 0, j)),
                  pl.BlockSpec((1, d, tf), lambda i, e, j, c: (e, 0, j)),
                  pl.BlockSpec((1, tf, d), lambda i, e, j, c: (e, j, 0)),
                  pl.BlockSpec((1, d), lambda i, e, j, c: (0, 0)), pl.BlockSpec((1, d), lambda i, e, j, c: (0, 0)),
                  pl.BlockSpec((tb, pd), lambda i, e, j, c: (i, 0), pipeline_mode=once),
                  pl.BlockSpec((d, d), lambda i, e, j, c: (0, 0), pipeline_mode=once),
                  pl.BlockSpec((1, d), lambda i, e, j, c: (0, 0)),
                  pl.BlockSpec((pd, d), lambda i, e, j, c: (0, 0), pipeline_mode=once)],
        out_specs=pl.BlockSpec((tb, d), lambda i, e, j, c: (i, 0), pipeline_mode=once),
        scratch_shapes=[pltpu.VMEM((tb, d), BF16), pltpu.VMEM((tb, d), BF16), pltpu.VMEM((tb, d), F32)],
    )
    return pl.pallas_call(
        _moe_body,
        grid_spec=grid_spec,
        out_shape=jax.ShapeDtypeStruct((t, d), F32),
        compiler_params=pltpu.CompilerParams(dimension_semantics=("parallel", "arbitrary", "arbitrary"),
                                             vmem_limit_bytes=MOE_VMEM_LIMIT_BYTES),
        name="moe",
    )(starts, x, comb, rank, rank_rows, wg, wu, wd, g.reshape(1, d), b.reshape(1, d), p, ple_wg, ple_bg.reshape(1, d),
      ple_wp)


def _moe_layer(h, w_router, wg, wu, wd, g, b, p, ple_wg, ple_bg, ple_wp, *, tb=2048):
    tb = min(tb, h.shape[0])
    comb, rank, starts = _router(h, w_router, tb=tb)
    return _moe(h, comb, rank, starts, wg.astype(BF16), wu.astype(BF16), wd.astype(BF16), g, b, p, ple_wg, ple_bg, ple_wp,
                tb=tb)


def _gmlp_gate_body(u_ref, v_ref, lng_ref, lnb_ref, ws_ref, bs_ref, wo_ref, h_ref, g_ref, b_ref, o_ref, gated_ref):
    tm, width = u_ref.shape
    gw = width // GM_GROUPS
    v = _layer_norm(v_ref[...].astype(F32), lng_ref[...], lnb_ref[...]).astype(BF16)
    mask = _tril(GM_CHUNK)
    for g in range(GM_GROUPS):
        ws = jnp.where(mask, ws_ref[g], 0.0).astype(BF16)
        bias = bs_ref[:, g:g + 1]
        for c in range(tm // GM_CHUNK):
            rows = slice(c * GM_CHUNK, (c + 1) * GM_CHUNK)
            cols = slice(g * gw, (g + 1) * gw)
            sv = _dot(ws, v[rows, cols]) + bias
            gated_ref[rows, cols] = (u_ref[rows, cols].astype(F32) * sv).astype(BF16)
    acc = _dot(gated_ref[...], wo_ref[...])
    o_ref[...] = _layer_norm(ALPHA * h_ref[...] + acc, g_ref[...], b_ref[...])


def _gmlp_mixer(h, w_in, b_in, ln_g, ln_b, w_s, b_s, w_out, g, b, *, tm=512):
    t, d = h.shape
    width = w_out.shape[0]
    tm = min(tm, t)
    z = _linear_gelu(h, w_in.astype(BF16), b_in)
    row = lambda i: (i, 0)
    const = lambda i: (0, 0)
    return pl.pallas_call(
        _gmlp_gate_body,
        grid=(t // tm,),
        in_specs=[pl.BlockSpec((tm, width), row), pl.BlockSpec((tm, width), lambda i: (i, 1)),
                  pl.BlockSpec((1, width), const), pl.BlockSpec((1, width), const),
                  pl.BlockSpec((GM_GROUPS, GM_CHUNK, GM_CHUNK), lambda i: (0, 0, 0)),
                  pl.BlockSpec((GM_CHUNK, GM_GROUPS), const), pl.BlockSpec((width, d), const),
                  pl.BlockSpec((tm, d), row), pl.BlockSpec((1, d), const), pl.BlockSpec((1, d), const)],
        out_specs=pl.BlockSpec((tm, d), row),
        out_shape=jax.ShapeDtypeStruct((t, d), F32),
        scratch_shapes=[pltpu.VMEM((tm, width), BF16)],
        compiler_params=_params("parallel"),
        name="gmlp_gate",
    )(z, z, ln_g.reshape(1, width), ln_b.reshape(1, width), w_s, b_s.T, w_out.astype(BF16), h,
      g.reshape(1, d), b.reshape(1, d))


def _fox_decay_body(f_ref, bf_ref, c_ref):
    s = f_ref.shape[1]
    x = f_ref[0] + bf_ref[...]
    log_f = jnp.minimum(x, 0.0) - jnp.log1p(jnp.exp(-jnp.abs(x)))
    tri = _tril(LANES).astype(F32)
    carry = jnp.zeros((1, LANES), F32)
    lane = lax.broadcasted_iota(jnp.int32, (LANES, LANES), 1)
    hds = FOX_HEADS
    for i in range(s // LANES):
        cs = _dot_hi(tri, log_f[i * LANES:(i + 1) * LANES]) + carry
        carry = cs[LANES - 1:LANES]
        hi = cs.astype(BF16).astype(F32)
        mid = (cs - hi).astype(BF16).astype(F32)
        lo = (cs - hi - mid).astype(BF16).astype(F32)
        out = jnp.where(lane < hds, hi, 0.0)
        for slot, piece in enumerate((mid, lo, -hi, -mid, -lo), start=1):
            moved = pltpu.roll(piece, slot * hds, 1)
            out = jnp.where((lane >= slot * hds) & (lane < (slot + 1) * hds), moved, out)
        c_ref[0, i * LANES:(i + 1) * LANES, :] = out.astype(c_ref.dtype)


def _fox_decay(f, b_f):
    bsz, s, _ = f.shape
    blk = lambda i: (i, 0, 0)
    return pl.pallas_call(
        _fox_decay_body,
        grid=(bsz,),
        in_specs=[pl.BlockSpec((1, s, LANES), blk), pl.BlockSpec((1, LANES), lambda i: (0, 0))],
        out_specs=pl.BlockSpec((1, s, LANES), blk),
        out_shape=jax.ShapeDtypeStruct((bsz, s, LANES), BF16),
        compiler_params=_params("parallel"),
        name="fox_decay",
    )(f, b_f)


FOX_BIAS_LANES = 6


def _fox_attn_body(q_ref, k_ref, v_ref, cq_ref, ck_ref, o_ref, k_s, v_s):
    grp, qi = pl.program_id(1), pl.program_id(2)
    tq = q_ref.shape[1]
    half = FOX_HEAD_DIM
    pieces = FOX_BIAS_LANES // 2
    heads = range(FOX_HEADS_PER_STEP)
    lane = lax.broadcasted_iota(jnp.int32, (1, LANES), 1)
    src = lax.broadcasted_iota(jnp.int32, (LANES, LANES), 0)
    dst = lax.broadcasted_iota(jnp.int32, (LANES, LANES), 1)
    scale = jnp.where(lane < half, half ** -0.5, 1.0).astype(BF16)

    def head_features(ref, hh):
        pair = ref[0, :, (hh // 2) * LANES:(hh // 2 + 1) * LANES]
        sel = jnp.where((dst < half) & (src == dst + half * (hh % 2)), 1.0, 0.0).astype(BF16)
        return _dot(pair, sel)

    def bias_lanes(c6, hh, first_piece, at, ones_at):
        head = grp * FOX_HEADS_PER_STEP + hh
        piece = dst - at + first_piece
        sel = jnp.where((dst >= at) & (dst < at + pieces) & (src == piece * FOX_HEADS + head), 1.0, 0.0).astype(BF16)
        return _dot(c6, sel) + jnp.where((lane >= ones_at) & (lane < ones_at + pieces), 1.0, 0.0)

    @pl.when(qi == 0)
    def _():
        ck = ck_ref[0]
        for hh in heads:
            k_s[hh] = (head_features(k_ref, hh) + bias_lanes(ck, hh, pieces, half + pieces, half)).astype(BF16)
            v_aug = head_features(v_ref, hh) + jnp.where(lane >= half, 1.0, 0.0)
            for j in range(v_s.shape[1]):
                v_s[hh, j] = v_aug[j * tq:(j + 1) * tq].T.astype(BF16)

    cq = cq_ref[0]
    qs = [(head_features(q_ref, hh) + bias_lanes(cq, hh, 0, half, half + pieces)).astype(BF16) * scale for hh in heads]
    kv_row = lax.broadcasted_iota(jnp.int32, (tq, tq), 0)
    q_col = lax.broadcasted_iota(jnp.int32, (tq, tq), 1)
    diag = kv_row <= q_col

    def scores(kj):
        rows = pl.ds(pl.multiple_of(kj * tq, tq), tq)
        return tuple(_dot_nt(k_s[hh, rows, :], qs[hh]) for hh in heads)

    def accumulate(kj, ss, carry, masked):
        if masked:
            ss = [jnp.where(diag, s, -1e30) for s in ss]
        ms = [jnp.maximum(carry[hh][0], jnp.max(ss[hh], axis=0, keepdims=True)) for hh in heads]
        ps = [jnp.exp(ss[hh] - ms[hh]).astype(BF16) for hh in heads]
        pvs = [_dot(v_s[hh, kj], ps[hh]) for hh in heads]
        return tuple((ms[hh], jnp.exp(carry[hh][0] - ms[hh]) * carry[hh][1] + pvs[hh]) for hh in heads)

    def block(kj, state):
        ss, carry = state
        return scores(kj + 1), accumulate(kj, ss, carry, masked=False)

    init = tuple((jnp.full((1, tq), -1e30, F32), jnp.zeros((LANES, tq), F32)) for _ in heads)
    ss, carry = lax.fori_loop(0, qi, block, (scores(0), init))
    carry = accumulate(qi, ss, carry, masked=True)
    outs = [(acc / acc[half:half + 1, :]).T for _, acc in carry]
    for pr in range(len(outs) // 2):
        pair = jnp.where(lane < half, outs[2 * pr], pltpu.roll(outs[2 * pr + 1], half, 1))
        o_ref[0, :, pr * LANES:(pr + 1) * LANES] = pair.astype(o_ref.dtype)


FOX_HEADS_PER_STEP = 4


def _fox_attention(qkv, c6, *, tq=256):
    bsz, s, w3 = qkv.shape
    tq = min(tq, s)
    hps = FOX_HEADS_PER_STEP
    gw = hps * FOX_HEAD_DIM
    groups = w3 // (3 * gw)
    return pl.pallas_call(
        _fox_attn_body,
        grid=(bsz, groups, s // tq),
        in_specs=[pl.BlockSpec((1, tq, gw), lambda b, h, i: (b, i, h)),
                  pl.BlockSpec((1, s, gw), lambda b, h, i: (b, 0, groups + h)),
                  pl.BlockSpec((1, s, gw), lambda b, h, i: (b, 0, 2 * groups + h)),
                  pl.BlockSpec((1, tq, LANES), lambda b, h, i: (b, i, 0)),
                  pl.BlockSpec((1, s, LANES), lambda b, h, i: (b, 0, 0))],
        out_specs=pl.BlockSpec((1, tq, gw), lambda b, h, i: (b, i, h)),
        out_shape=jax.ShapeDtypeStruct((bsz, s, w3 // 3), BF16),
        scratch_shapes=[pltpu.VMEM((hps, s, LANES), BF16), pltpu.VMEM((hps, s // tq, LANES, tq), BF16)],
        compiler_params=_params("parallel", "parallel", "arbitrary"),
        name="fox_attn",
    )(qkv, qkv, qkv, c6, c6)


def _fox_mixer(h, bsz, w_in, b_f, w_out, g, b):
    t, d = h.shape
    s = t // bsz
    hds, dh = FOX_HEADS, FOX_HEAD_DIM
    width = hds * dh
    qkv = _linear(h, w_in[:, :3 * width].astype(BF16))
    w_f = jnp.pad(w_in[:, 3 * width:], ((0, 0), (0, LANES - hds)))
    f = _linear(h, w_f, hi=True, out_dtype=F32, tn=LANES)
    b_pad = jnp.pad(b_f, (0, LANES - hds)).reshape(1, LANES)
    c6 = _fox_decay(f.reshape(bsz, s, LANES), b_pad)
    o = _fox_attention(qkv.reshape(bsz, s, 3 * width), c6)
    return _proj_ln(o.reshape(t, width), w_out.astype(BF16), h, g, b)


def _causal_conv(x, taps):
    k = len(taps)
    row = lax.broadcasted_iota(jnp.int32, x.shape, 0)
    out = x * taps[k - 1]
    for sh in range(1, k):
        shifted = jnp.where(row >= sh, pltpu.roll(x, sh, 0), 0.0)
        out = out + shifted * taps[k - 1 - sh]
    return out


def _gdn_gates_body(x_ref, dtb_ref, alog_ref, o_ref):
    x = x_ref[...]
    lane = lax.broadcasted_iota(jnp.int32, x.shape, 1)
    beta = _sigmoid(x)
    gdec = -jnp.exp(alog_ref[...]) * _softplus(x + dtb_ref[...])
    gdec = jnp.where((lane >= GDN_HEADS) & (lane < 2 * GDN_HEADS), gdec, 0.0)
    tri = _tril(GDN_CHUNK).astype(F32)
    lane_c = lax.broadcasted_iota(jnp.int32, (GDN_CHUNK, LANES), 1)
    for c in range(x.shape[0] // GDN_CHUNK):
        rows = slice(c * GDN_CHUNK, (c + 1) * GDN_CHUNK)
        gc = _dot_hi(tri, gdec[rows])
        o_ref[rows, :] = jnp.where(lane_c < GDN_HEADS, beta[rows], gc)


def _gdn_gates(x, dt_bias, a_log, *, tm=512):
    t = x.shape[0]
    tm = min(tm, t)
    pad = lambda v: jnp.pad(v, (GDN_HEADS, LANES - 2 * GDN_HEADS)).reshape(1, LANES)
    return pl.pallas_call(
        _gdn_gates_body,
        grid=(t // tm,),
        in_specs=[pl.BlockSpec((tm, LANES), lambda i: (i, 0)), pl.BlockSpec((1, LANES), lambda i: (0, 0)),
                  pl.BlockSpec((1, LANES), lambda i: (0, 0))],
        out_specs=pl.BlockSpec((tm, LANES), lambda i: (i, 0)),
        out_shape=jax.ShapeDtypeStruct((t, LANES), F32),
        compiler_params=_params("parallel"),
        name="gdn_gates",
    )(x, pad(dt_bias), pad(a_log))


def _unit_lower_inverses(ms):
    n = ms[0].shape[0]
    eye = (lax.broadcasted_iota(jnp.int32, (n, n), 0) == lax.broadcasted_iota(jnp.int32, (n, n), 1)).astype(F32)
    ps = [-m for m in ms]
    invs = [eye + p for p in ps]
    step = 1
    while 2 * step < n:
        ps = [_dot_split(p, p) for p in ps]
        invs = [inv + _dot_split(inv, p) for inv, p in zip(invs, ps)]
        step *= 2
    return invs


GDN_HEADS_PER_STEP = 2
GDN_CHUNKS_PER_ITER = 4


def _gdn_body(q_ref, k_ref, v_ref, gate_ref, cw_ref, gcol_ref, grow_ref, ng_ref, o_ref,
              q_s, k_s, kb_s, uw_s, g_s, a_s, o_s):
    hb = GDN_HEADS_PER_STEP
    grp = GDN_CHUNKS_PER_ITER
    hd0 = pl.program_id(1) * hb
    s = q_ref.shape[1]
    c = GDN_CHUNK
    dk = GDN_DK
    n_chunks = s // c

    gcols = gcol_ref[0]
    lane = lax.broadcasted_iota(jnp.int32, gcols.shape, 1)
    for hh in range(hb):
        cols = slice(hh * dk, (hh + 1) * dk)

        def conv_act(x_ref, which):
            taps = [cw_ref[hh, which, i:i + 1, :] for i in range(cw_ref.shape[2])]
            return _silu(_causal_conv(x_ref[0, :, cols].astype(F32), taps))

        q = conv_act(q_ref, 0)
        k = conv_act(k_ref, 1)
        v = conv_act(v_ref, 2)
        q = q * lax.rsqrt(jnp.sum(q * q, axis=-1, keepdims=True) + L2_EPS) * (dk ** -0.5)
        k = k * lax.rsqrt(jnp.sum(k * k, axis=-1, keepdims=True) + L2_EPS)
        beta = jnp.sum(jnp.where(lane == hd0 + hh, gcols, 0.0), axis=-1, keepdims=True)
        gc = jnp.sum(jnp.where(lane == hd0 + hh + GDN_HEADS, gcols, 0.0), axis=-1, keepdims=True)
        kb = k * beta
        q_s[hh] = q
        k_s[hh] = k
        kb_s[hh] = kb
        uw_s[hh, :, 0:dk] = v * beta
        uw_s[hh, :, dk:2 * dk] = kb * jnp.exp(gc)
        g_s[hh] = jnp.broadcast_to(gc, g_s.shape[1:])
    causal = _tril(c)
    strict = _tril(c, strict=True)

    def prep(it, carry):
        base = pl.multiple_of(it * (grp * c), grp * c)
        where, ms, aqks = [], [], []
        for hh in range(hb):
            for gi in range(grp):
                rows = pl.ds(base + gi * c, c)
                kc = k_s[hh, rows, :]
                gcc = g_s[hh, rows, 0:1]
                grow = grow_ref[0, hh, pl.ds(it * grp + gi, 1), :]
                decay = jnp.exp(jnp.where(causal, gcc - grow, -jnp.inf))
                ms.append(jnp.where(strict, _dot_nt(kb_s[hh, rows, :], kc) * decay, 0.0))
                aqks.append(_dot_nt(q_s[hh, rows, :], kc) * decay)
                where.append((hh, gi, rows))
        invs = _unit_lower_inverses(ms)
        uws = [_dot_split(inv, uw_s[hh, rows, :]) for inv, (hh, gi, rows) in zip(invs, where)]
        for (hh, gi, rows), aqk, uw in zip(where, aqks, uws):
            a_s[hh, it * grp + gi] = aqk
            uw_s[hh, rows, :] = uw
            gcc = g_s[hh, rows, 0:1]
            g_last = g_s[hh, pl.ds(base + gi * c + c - 1, 1), 0:1]
            q_s[hh, rows, :] = q_s[hh, rows, :] * jnp.exp(gcc)
            k_s[hh, rows, :] = k_s[hh, rows, :] * jnp.exp(g_last - gcc)
        return carry

    lax.fori_loop(0, n_chunks // grp, prep, 0)

    def scan(i, states):
        rows = pl.ds(pl.multiple_of(i * c, c), c)
        heads = range(hb)
        v_news = [uw_s[hh, rows, 0:dk] - _dot(uw_s[hh, rows, dk:2 * dk], states[hh]) for hh in heads]
        for hh in heads:
            o_s[hh, rows, :] = _dot(q_s[hh, rows, :], states[hh]) + _dot(a_s[hh, i], v_news[hh])
        decays = [jnp.exp(g_s[hh, pl.ds(i * c + c - 1, 1), 0:1]) for hh in heads]
        return tuple(states[hh] * decays[hh] + _dot_tn(k_s[hh, rows, :], v_news[hh]) for hh in heads)

    lax.fori_loop(0, n_chunks, scan, tuple(jnp.zeros((dk, dk), F32) for _ in range(hb)))
    for hh in range(hb):
        cols = slice(hh * dk, (hh + 1) * dk)
        o = o_s[hh]
        o = o * lax.rsqrt(jnp.mean(o * o, axis=-1, keepdims=True) + RMS_EPS) * ng_ref[...]
        o_ref[0, :, cols] = (o * _silu(gate_ref[0, :, cols].astype(F32))).astype(o_ref.dtype)


def _gdn_core(proj, conv_w, gcol, grow, norm_g):
    bsz, s, _ = proj.shape
    hd, dk, c, hb = GDN_HEADS, GDN_DK, GDN_CHUNK, GDN_HEADS_PER_STEP
    n = s // c
    steps = hd // hb
    cw = conv_w.reshape(conv_w.shape[0], 3, hd, dk).transpose(2, 1, 0, 3)
    kk = conv_w.shape[0]
    seq = lambda off: pl.BlockSpec((1, s, hb * dk), lambda b, h: (b, 0, off + h))
    per_head = lambda *shape: pltpu.VMEM((hb,) + shape, F32)
    return pl.pallas_call(
        _gdn_body,
        grid=(bsz, steps),
        in_specs=[seq(0), seq(steps), seq(2 * steps), seq(3 * steps),
                  pl.BlockSpec((hb, 3, kk, dk), lambda b, h: (h, 0, 0, 0)),
                  pl.BlockSpec((1, s, LANES), lambda b, h: (b, 0, 0)),
                  pl.BlockSpec((1, hb, n, c), lambda b, h: (b, h, 0, 0)),
                  pl.BlockSpec((1, dk), lambda b, h: (0, 0))],
        out_specs=seq(0),
        out_shape=jax.ShapeDtypeStruct((bsz, s, hd * dk), BF16),
        scratch_shapes=[per_head(s, dk), per_head(s, dk), per_head(s, dk), per_head(s, 2 * dk), per_head(s, LANES),
                        per_head(n, c, c), per_head(s, dk)],
        compiler_params=_params("parallel", "arbitrary"),
        name="gdn_core",
    )(proj, proj, proj, proj, cw, gcol, grow, norm_g.reshape(1, dk))


def _gdn_mixer(h, bsz, w_in, conv_w, a_log, dt_bias, norm_g, w_out, g, b):
    t, d = h.shape
    s = t // bsz
    hd = GDN_HEADS
    qkv_w = 3 * hd * GDN_DK
    gate_w = hd * GDN_DK
    proj = _linear(h, w_in[:, :qkv_w + gate_w].astype(BF16))
    w_small = jnp.pad(w_in[:, qkv_w + gate_w:], ((0, 0), (0, LANES - 2 * hd)))
    logits = _linear(h, w_small, hi=True, out_dtype=F32, tn=LANES)
    gcol = _gdn_gates(logits, dt_bias, a_log).reshape(bsz, s, LANES)
    grow = gcol[:, :, hd:2 * hd].transpose(0, 2, 1).reshape(bsz, hd, s // GDN_CHUNK, GDN_CHUNK)
    o = _gdn_core(proj.reshape(bsz, s, qkv_w + gate_w), conv_w, gcol, grow, norm_g)
    return _proj_ln(o.reshape(t, gate_w), w_out.astype(BF16), h, g, b)


def _ssd_gates_body(x_ref, dtb_ref, alog_ref, o_ref):
    x = x_ref[...]
    lane = lax.broadcasted_iota(jnp.int32, x.shape, 1)
    dt = _softplus(x + dtb_ref[...])
    a = jnp.where(lane < SSD_HEADS, dt * (-jnp.exp(alog_ref[...])), 0.0)
    tri = _tril(SSD_CHUNK).astype(F32)
    lane_c = lax.broadcasted_iota(jnp.int32, (SSD_CHUNK, LANES), 1)
    for c in range(x.shape[0] // SSD_CHUNK):
        rows = slice(c * SSD_CHUNK, (c + 1) * SSD_CHUNK)
        acum = pltpu.roll(_dot_hi(tri, a[rows]), SSD_HEADS, 1)
        o_ref[rows, :] = jnp.where(lane_c < SSD_HEADS, dt[rows], acum)


def _ssd_gates(x, dt_bias, a_log, *, tm=512):
    t = x.shape[0]
    tm = min(tm, t)
    pad = lambda v: jnp.pad(v, (0, LANES - SSD_HEADS)).reshape(1, LANES)
    return pl.pallas_call(
        _ssd_gates_body,
        grid=(t // tm,),
        in_specs=[pl.BlockSpec((tm, LANES), lambda i: (i, 0)), pl.BlockSpec((1, LANES), lambda i: (0, 0)),
                  pl.BlockSpec((1, LANES), lambda i: (0, 0))],
        out_specs=pl.BlockSpec((tm, LANES), lambda i: (i, 0)),
        out_shape=jax.ShapeDtypeStruct((t, LANES), F32),
        compiler_params=_params("parallel"),
        name="ssd_gates",
    )(x, pad(dt_bias), pad(a_log))


def _ssd_body(x_ref, b_ref, c_ref, z_ref, cwx_ref, cwb_ref, cwc_ref, cbx_ref, cbb_ref, cbc_ref,
              gcol_ref, arow_ref, d_ref, ng_ref, o_ref, x_s, b_s, c_s, y_s):
    grp = pl.program_id(1)
    s = x_ref.shape[1]
    c = SSD_CHUNK
    hp = SSD_HEAD_DIM
    heads = x_ref.shape[2] // hp
    n_chunks = s // c

    def conv_act(ref, w_ref, bias_ref):
        taps = [w_ref[i:i + 1, :] for i in range(w_ref.shape[0])]
        return _silu(_causal_conv(ref[0].astype(F32), taps) + bias_ref[...])

    x_s[...] = conv_act(x_ref, cwx_ref, cbx_ref)
    b_s[...] = conv_act(b_ref, cwb_ref, cbb_ref)
    c_s[...] = conv_act(c_ref, cwc_ref, cbc_ref)
    causal = _tril(c)
    pair_lane = lax.broadcasted_iota(jnp.int32, (c, 2 * hp), 1) < hp

    def head_cols(gcol, base):
        lane = lax.broadcasted_iota(jnp.int32, gcol.shape, 1)
        return [jnp.sum(jnp.where(lane == base + grp * heads + j, gcol, 0.0), axis=-1, keepdims=True)
                for j in range(heads)]

    def widen(cols):
        pairs = [jnp.where(pair_lane, cols[2 * p], cols[2 * p + 1]) for p in range(heads // 2)]
        return jnp.concatenate(pairs, axis=1)

    def step(i, state):
        rows = pl.ds(pl.multiple_of(i * c, c), c)
        gcol = gcol_ref[0, rows, :]
        dt = head_cols(gcol, 0)
        acum = head_cols(gcol, SSD_HEADS)
        arow = arow_ref[0, 0, i]
        xc = x_s[rows, :]
        bc = b_s[rows, :]
        cc = c_s[rows, :]
        xdt = xc * widen(dt)
        cb = _dot_nt(cc, bc)
        y = _dot(cc, state) * widen([jnp.exp(a) for a in acum])
        intra = []
        for p in range(heads // 2):
            xp = xdt[:, 2 * p * hp:(2 * p + 2) * hp]
            ys = []
            for j in (2 * p, 2 * p + 1):
                lmat = jnp.exp(jnp.where(causal, acum[j] - arow[j:j + 1, :], -jnp.inf))
                ys.append(_dot(cb * lmat, xp))
            intra.append(jnp.where(pair_lane, ys[0], ys[1]))
        y = y + jnp.concatenate(intra, axis=1) + xc * d_ref[...]
        y_s[rows, :] = y
        last = head_cols(gcol_ref[0, pl.ds(i * c + c - 1, 1), :], SSD_HEADS)
        xdec = xdt * widen([jnp.exp(last[j] - acum[j]) for j in range(heads)])
        first_lane = lax.broadcasted_iota(jnp.int32, (1, 2 * hp), 1) < hp
        lastw = jnp.concatenate([jnp.where(first_lane, jnp.exp(last[2 * p]), jnp.exp(last[2 * p + 1]))
                                 for p in range(heads // 2)], axis=1)
        return state * lastw + _dot_tn(bc, xdec)

    lax.fori_loop(0, n_chunks, step, jnp.zeros((SSD_STATE, x_ref.shape[2]), F32))
    y = y_s[...] * _silu(z_ref[0].astype(F32))
    y = y * lax.rsqrt(jnp.mean(y * y, axis=-1, keepdims=True) + RMS_EPS) * ng_ref[...]
    o_ref[0] = y.astype(o_ref.dtype)


def _ssd_core(proj, conv_w, conv_b, gcol, arow, d_wide, norm_g):
    bsz, s, _ = proj.shape
    grp, n_state, c = SSD_GROUPS, SSD_STATE, SSD_CHUNK
    inner = SSD_HEADS * SSD_HEAD_DIM
    gw = inner // grp
    heads = SSD_HEADS // grp
    n = s // c
    kk = conv_w.shape[0]
    xblk = lambda off: pl.BlockSpec((1, s, gw), lambda b, g: (b, 0, off + g))
    nblk = lambda off: pl.BlockSpec((1, s, n_state), lambda b, g: (b, 0, off + g))
    wx = lambda rows, off: pl.BlockSpec((rows, gw), lambda b, g: (0, off + g))
    wn = lambda rows, off: pl.BlockSpec((rows, n_state), lambda b, g: (0, off + g))
    x_off = inner // gw
    b_off = 2 * inner // n_state
    c_off = b_off + grp
    return pl.pallas_call(
        _ssd_body,
        grid=(bsz, grp),
        in_specs=[xblk(x_off), nblk(b_off), nblk(c_off), xblk(0),
                  wx(kk, 0), wn(kk, inner // n_state), wn(kk, inner // n_state + grp),
                  wx(1, 0), wn(1, inner // n_state), wn(1, inner // n_state + grp),
                  pl.BlockSpec((1, s, LANES), lambda b, g: (b, 0, 0)),
                  pl.BlockSpec((1, 1, n, heads, c), lambda b, g: (b, g, 0, 0, 0)),
                  wx(1, 0), wx(1, 0)],
        out_specs=xblk(0),
        out_shape=jax.ShapeDtypeStruct((bsz, s, inner), BF16),
        scratch_shapes=[pltpu.VMEM((s, gw), F32), pltpu.VMEM((s, n_state), F32), pltpu.VMEM((s, n_state), F32),
                        pltpu.VMEM((s, gw), F32)],
        compiler_params=_params("parallel", "arbitrary"),
        name="ssd_core",
    )(proj, proj, proj, proj, conv_w, conv_w, conv_w, conv_b, conv_b, conv_b, gcol, arow, d_wide, norm_g)


def _ssd_mixer(h, bsz, w_in, conv_w, conv_b, dt_bias, a_log, d_skip, norm_g, w_out, g, b):
    t, d = h.shape
    s = t // bsz
    inner = SSD_HEADS * SSD_HEAD_DIM
    main = 2 * inner + 2 * SSD_GROUPS * SSD_STATE
    heads = SSD_HEADS // SSD_GROUPS
    proj = _linear(h, w_in[:, :main].astype(BF16))
    w_dt = jnp.pad(w_in[:, main:], ((0, 0), (0, LANES - SSD_HEADS)))
    dt_logits = _linear(h, w_dt, hi=True, out_dtype=F32, tn=LANES)
    gcol = _ssd_gates(dt_logits, dt_bias, a_log).reshape(bsz, s, LANES)
    arow = gcol[:, :, SSD_HEADS:2 * SSD_HEADS].reshape(bsz, s // SSD_CHUNK, SSD_CHUNK, SSD_GROUPS, heads)
    arow = arow.transpose(0, 3, 1, 4, 2)
    d_wide = jnp.repeat(d_skip, SSD_HEAD_DIM).reshape(1, inner)
    o = _ssd_core(proj.reshape(bsz, s, main), conv_w, conv_b.reshape(1, -1), gcol, arow, d_wide,
                  norm_g.reshape(1, inner))
    return _proj_ln(o.reshape(t, inner), w_out.astype(BF16), h, g, b)


def kernel(x, p, ln_g, ln_b, gm_w_in, gm_b_in, gm_ln_g, gm_ln_b, gm_w_s, gm_b_s, gm_w_out, fox_w_in, fox_b_f, fox_w_out, gdn_w_in, gdn_conv_w, gdn_a_log, gdn_dt_bias, gdn_norm_g, gdn_w_out, ssd_w_in, ssd_conv_w, ssd_conv_b, ssd_dt_bias, ssd_a_log, ssd_d, ssd_norm_g, ssd_w_out, ffn_w_gate, ffn_w_up, ffn_w_down, moe_w_router, moe_w_gate, moe_w_up, moe_w_down, ple_w_gate, ple_b_gate, ple_w_proj):
    bsz, s, d = x.shape
    t = bsz * s
    h = x.reshape(t, d)
    pf = p.reshape(p.shape[0], t, p.shape[-1])
    for i in range(DEPTH):
        m, j = i % 4, i // 4
        g0, b0 = ln_g[i, 0], ln_b[i, 0]
        if m == 0:
            h = _gmlp_mixer(h, gm_w_in[j], gm_b_in[j], gm_ln_g[j], gm_ln_b[j], gm_w_s[j], gm_b_s[j], gm_w_out[j], g0, b0)
        elif m == 1:
            h = _fox_mixer(h, bsz, fox_w_in[j], fox_b_f[j], fox_w_out[j], g0, b0)
        elif m == 2:
            h = _gdn_mixer(h, bsz, gdn_w_in[j], gdn_conv_w[j], gdn_a_log[j], gdn_dt_bias[j], gdn_norm_g[j], gdn_w_out[j],
                           g0, b0)
        else:
            h = _ssd_mixer(h, bsz, ssd_w_in[j], ssd_conv_w[j], ssd_conv_b[j], ssd_dt_bias[j], ssd_a_log[j], ssd_d[j],
                           ssd_norm_g[j], ssd_w_out[j], g0, b0)
        g1, b1 = ln_g[i, 1], ln_b[i, 1]
        ple = (pf[i], ple_w_gate[i].astype(BF16), ple_b_gate[i], ple_w_proj[i].astype(BF16))
        if i % 2 == 0:
            h = _ffn(h, ffn_w_gate[i // 2].astype(BF16), ffn_w_up[i // 2].astype(BF16), ffn_w_down[i // 2].astype(BF16),
                     g1, b1, *ple)
        else:
            h = _moe_layer(h, moe_w_router[i // 2], moe_w_gate[i // 2], moe_w_up[i // 2], moe_w_down[i // 2], g1, b1, *ple)
    return h.reshape(bsz, s, d)
```

```python
import functools
import math

import jax
import jax.numpy as jnp
from jax import lax
from jax.experimental import pallas as pl
from jax.experimental.pallas import tpu as pltpu

F32 = jnp.float32
BF16 = jnp.bfloat16
HIGHEST = lax.Precision.HIGHEST

DEPTH = 4
ALPHA = (2 * DEPTH) ** 0.25
LN_EPS = 1e-5
RMS_EPS = 1e-6
L2_EPS = 1e-6

GM_CHUNK = 128
GM_GROUPS = 8
FOX_HEADS = 16
FOX_HEAD_DIM = 64
GDN_HEADS = 8
GDN_DK = 128
GDN_CHUNK = 64
SSD_HEADS = 32
SSD_HEAD_DIM = 64
SSD_GROUPS = 4
SSD_STATE = 128
SSD_CHUNK = 64
N_EXPERTS = 8
LANES = 128

VMEM_LIMIT_BYTES = 56 * 1024 * 1024
MOE_VMEM_LIMIT_BYTES = 60 * 1024 * 1024


def _params(*sem):
    return pltpu.CompilerParams(dimension_semantics=sem, vmem_limit_bytes=VMEM_LIMIT_BYTES)


def _dot(a, b):
    return jnp.dot(a, b, preferred_element_type=F32)


def _dot_nt(a, b):
    return lax.dot_general(a, b, (((1,), (1,)), ((), ())), preferred_element_type=F32)


def _dot_tn(a, b):
    return lax.dot_general(a, b, (((0,), (0,)), ((), ())), preferred_element_type=F32)


def _dot_hi(a, b):
    return jnp.dot(a, b, preferred_element_type=F32, precision=HIGHEST)


def _split_bf16(a):
    hi = a.astype(BF16)
    return hi, (a - hi.astype(F32)).astype(BF16)


def _dot_split(a, b):
    ah, al = _split_bf16(a)
    bh, bl = _split_bf16(b)
    return _dot(ah, bh) + (_dot(ah, bl) + _dot(al, bh))


def _layer_norm(r, g, b):
    mu = jnp.mean(r, axis=-1, keepdims=True)
    d = r - mu
    var = jnp.mean(d * d, axis=-1, keepdims=True)
    return d * lax.rsqrt(var + LN_EPS) * g + b


def _sigmoid(x):
    return 1.0 / (1.0 + jnp.exp(-x))


def _silu(x):
    return x * _sigmoid(x)


def _softplus(x):
    return jnp.maximum(x, 0.0) + jnp.log1p(jnp.exp(-jnp.abs(x)))


def _tril(n, strict=False):
    r = lax.broadcasted_iota(jnp.int32, (n, n), 0)
    c = lax.broadcasted_iota(jnp.int32, (n, n), 1)
    return (r > c) if strict else (r >= c)


def _linear_body(x_ref, w_ref, o_ref, *, hi):
    if hi:
        y = _dot_split(x_ref[...], w_ref[...])
    else:
        y = _dot(x_ref[...].astype(BF16), w_ref[...])
    o_ref[...] = y.astype(o_ref.dtype)


def _linear(x, w, *, hi=False, out_dtype=BF16, tm=1024, tn=1024):
    t, k = x.shape
    n = w.shape[1]
    tm, tn = min(tm, t), min(tn, n)
    return pl.pallas_call(
        functools.partial(_linear_body, hi=hi),
        grid=(t // tm, n // tn),
        in_specs=[pl.BlockSpec((tm, k), lambda i, j: (i, 0)), pl.BlockSpec((k, tn), lambda i, j: (0, j))],
        out_specs=pl.BlockSpec((tm, tn), lambda i, j: (i, j)),
        out_shape=jax.ShapeDtypeStruct((t, n), out_dtype),
        compiler_params=_params("parallel", "arbitrary"),
        name="linear_hi" if hi else "linear",
    )(x, w)


def _linear_gelu_body(x_ref, w_ref, b_ref, o_ref):
    y = _dot(x_ref[...].astype(BF16), w_ref[...]) + b_ref[...]
    o_ref[...] = jax.nn.gelu(y).astype(o_ref.dtype)


def _linear_gelu(x, w, b, *, tm=1024, tn=1024):
    t, k = x.shape
    n = w.shape[1]
    tm = min(tm, t)
    return pl.pallas_call(
        _linear_gelu_body,
        grid=(t // tm, n // tn),
        in_specs=[pl.BlockSpec((tm, k), lambda i, j: (i, 0)), pl.BlockSpec((k, tn), lambda i, j: (0, j)),
                  pl.BlockSpec((1, tn), lambda i, j: (0, j))],
        out_specs=pl.BlockSpec((tm, tn), lambda i, j: (i, j)),
        out_shape=jax.ShapeDtypeStruct((t, n), BF16),
        compiler_params=_params("parallel", "arbitrary"),
        name="linear_gelu",
    )(x, w, b.reshape(1, n))


def _proj_ln_body(y_ref, w_ref, h_ref, g_ref, b_ref, o_ref):
    acc = _dot(y_ref[...].astype(BF16), w_ref[...])
    o_ref[...] = _layer_norm(ALPHA * h_ref[...] + acc, g_ref[...], b_ref[...])


def _proj_ln(y, w, h, g, b, *, tm=512):
    t, k = y.shape
    d = w.shape[1]
    tm = min(tm, t)
    row = lambda i: (i, 0)
    const = lambda i: (0, 0)
    return pl.pallas_call(
        _proj_ln_body,
        grid=(t // tm,),
        in_specs=[pl.BlockSpec((tm, k), row), pl.BlockSpec((k, d), const), pl.BlockSpec((tm, d), row),
                  pl.BlockSpec((1, d), const), pl.BlockSpec((1, d), const)],
        out_specs=pl.BlockSpec((tm, d), row),
        out_shape=jax.ShapeDtypeStruct((t, d), F32),
        compiler_params=_params("parallel"),
        name="proj_ln",
    )(y, w, h, g.reshape(1, d), b.reshape(1, d))


def _norm_then_ple(r, g_ref, b_ref, p_ref, pwg_ref, pbg_ref, pwp_ref):
    h = _layer_norm(r, g_ref[...], b_ref[...])
    gate = _sigmoid(_dot(h.astype(BF16), pwg_ref[...]) + pbg_ref[...])
    return h + gate * _dot(p_ref[...].astype(BF16), pwp_ref[...])


def _ffn_body(x_ref, wg_ref, wu_ref, wd_ref, g_ref, b_ref, p_ref, pwg_ref, pbg_ref, pwp_ref, o_ref, acc_ref):
    j = pl.program_id(1)

    @pl.when(j == 0)
    def _():
        acc_ref[...] = jnp.zeros_like(acc_ref)

    x = x_ref[...].astype(BF16)
    a = _dot(x, wg_ref[0])
    u = _dot(x, wu_ref[0])
    acc_ref[...] += _dot((_silu(a) * u).astype(BF16), wd_ref[0])

    @pl.when(j == pl.num_programs(1) - 1)
    def _():
        o_ref[...] = _norm_then_ple(ALPHA * x_ref[...] + acc_ref[...], g_ref, b_ref, p_ref.at[0], pwg_ref, pbg_ref,
                                    pwp_ref)


def _ffn(x, wg, wu, wd, layer, g, b, p, p_layer, ple_wg, ple_bg, ple_wp, *, tm=1024, tf=512):
    t, d = x.shape
    ff = wg.shape[2]
    pd = p.shape[2]
    tm = min(tm, t)
    const = lambda i, j: (0, 0)
    return pl.pallas_call(
        _ffn_body,
        grid=(t // tm, ff // tf),
        in_specs=[pl.BlockSpec((tm, d), lambda i, j: (i, 0)), pl.BlockSpec((1, d, tf), lambda i, j: (layer, 0, j)),
                  pl.BlockSpec((1, d, tf), lambda i, j: (layer, 0, j)), pl.BlockSpec((1, tf, d), lambda i, j: (layer, j, 0)),
                  pl.BlockSpec((1, d), const), pl.BlockSpec((1, d), const),
                  pl.BlockSpec((1, tm, pd), lambda i, j: (p_layer, i, 0)), pl.BlockSpec((d, d), const),
                  pl.BlockSpec((1, d), const), pl.BlockSpec((pd, d), const)],
        out_specs=pl.BlockSpec((tm, d), lambda i, j: (i, 0)),
        out_shape=jax.ShapeDtypeStruct((t, d), F32),
        scratch_shapes=[pltpu.VMEM((tm, d), F32)],
        compiler_params=_params("parallel", "arbitrary"),
        name="ffn",
    )(x, wg, wu, wd, g.reshape(1, d), b.reshape(1, d), p, ple_wg, ple_bg.reshape(1, d), ple_wp)


def _router_body(x_ref, w_ref, comb_ref, rank_ref, start_ref):
    logits = _dot_split(x_ref[...], w_ref[...])
    lane = lax.broadcasted_iota(jnp.int32, logits.shape, 1)
    neg = jnp.float32(-jnp.inf)
    logits = jnp.where(lane < N_EXPERTS, logits, neg)
    m1 = jnp.max(logits, axis=-1, keepdims=True)
    i1 = jnp.min(jnp.where(logits == m1, lane, LANES), axis=-1, keepdims=True)
    rest = jnp.where(lane == i1, neg, logits)
    m2 = jnp.max(rest, axis=-1, keepdims=True)
    i2 = jnp.min(jnp.where(rest == m2, lane, LANES), axis=-1, keepdims=True)
    e2 = jnp.exp(m2 - m1)
    w1 = 1.0 / (1.0 + e2)
    w2 = e2 / (1.0 + e2)
    comb_ref[...] = jnp.where(lane == i1, w1, 0.0) + jnp.where(lane == i2, w2, 0.0)
    chosen = jnp.where((lane == i1) | (lane == i2), 1.0, 0.0)
    tb = logits.shape[0]
    sub = MOE_SUB
    earlier = jnp.where(_tril(sub, strict=True), 1.0, 0.0).astype(BF16)
    before = jnp.zeros((1, LANES), F32)
    for s in range(tb // sub):
        rows = slice(s * sub, (s + 1) * sub)
        picked = chosen[rows]
        counts = _dot(earlier, picked.astype(BF16)) + before
        rank_ref[rows, :] = jnp.where(picked > 0.0, counts.astype(jnp.int32), -1)
        start_ref[0, s:s + 1, :] = before.astype(jnp.int32)
        before = before + jnp.sum(picked, axis=0, keepdims=True)
    start_ref[0, tb // sub:tb // sub + 1, :] = before.astype(jnp.int32)


def _router(x, w_router, *, tb):
    t, d = x.shape
    w = jnp.pad(w_router, ((0, 0), (0, LANES - w_router.shape[1])))
    row = lambda i: (i, 0)
    marks = tb // MOE_SUB + 1
    return pl.pallas_call(
        _router_body,
        grid=(t // tb,),
        in_specs=[pl.BlockSpec((tb, d), row), pl.BlockSpec((d, LANES), lambda i: (0, 0))],
        out_specs=[pl.BlockSpec((tb, LANES), row), pl.BlockSpec((tb, LANES), row),
                   pl.BlockSpec((1, marks, LANES), lambda i: (i, 0, 0))],
        out_shape=[jax.ShapeDtypeStruct((t, LANES), F32), jax.ShapeDtypeStruct((t, LANES), jnp.int32),
                   jax.ShapeDtypeStruct((t // tb, marks, LANES), jnp.int32)],
        compiler_params=_params("parallel"),
        name="router",
    )(x, w)


MOE_ROW_TILE = 128
MOE_MOVE_TILE = 256
MOE_SUB = 256


def _moe_body(start_ref, x_ref, comb_ref, rcol_ref, rrow_ref, wg_ref, wu_ref, wd_ref, g_ref, b_ref,
              p_ref, pwg_ref, pbg_ref, pwp_ref, o_ref, xb_s, xg_s, y_s):
    blk, e, j = pl.program_id(0), pl.program_id(1), pl.program_id(2)
    tm, tg, sub = MOE_ROW_TILE, MOE_MOVE_TILE, MOE_SUB
    tb, d = x_ref.shape
    n_sub = tb // sub
    base = (blk * pl.num_programs(1) + e) * (n_sub + 1)
    starts = [start_ref[base + s] for s in range(n_sub + 1)]
    n_tiles = (starts[n_sub] + (tm - 1)) // tm
    n_moves = (starts[n_sub] + (tg - 1)) // tg

    @pl.when((e == 0) & (j == 0))
    def _():
        xb_s[...] = x_ref[...].astype(BF16)
        o_ref[...] = jnp.zeros_like(o_ref)

    def tile_rows(t, size):
        return pl.ds(pl.multiple_of(t * size, size), size)

    def overlaps(s, lo):
        return (starts[s] < lo + tg) & (starts[s + 1] > lo)

    @pl.when(j == 0)
    def _():
        row = lax.broadcasted_iota(jnp.int32, (tg, sub), 0)

        def gather(t, carry):
            lo = t * tg
            rows = tile_rows(t, tg)
            y_s[rows, :] = jnp.zeros((tg, d), F32)
            for s in range(n_sub):
                @pl.when(overlaps(s, lo))
                def _():
                    slot_of_token = rrow_ref[0, pl.ds(e, 1), s * sub:(s + 1) * sub]
                    onehot = jnp.where(slot_of_token - lo == row, 1.0, 0.0).astype(BF16)
                    y_s[rows, :] += _dot(onehot, xb_s[s * sub:(s + 1) * sub, :])
            xg_s[rows, :] = y_s[rows, :].astype(BF16)
            y_s[rows, :] = jnp.zeros((tg, d), F32)
            return carry

        lax.fori_loop(0, n_moves, gather, 0)

    def expert(t, carry):
        rows = tile_rows(t, tm)
        xt = xg_s[rows, :]
        a = _dot(xt, wg_ref[0, 0])
        u = _dot(xt, wu_ref[0, 0])
        y_s[rows, :] += _dot((_silu(a) * u).astype(BF16), wd_ref[0, 0])
        return carry

    lax.fori_loop(0, n_tiles, expert, 0)

    @pl.when(j == pl.num_programs(2) - 1)
    def _():
        lane = lax.broadcasted_iota(jnp.int32, (sub, LANES), 1)
        col = lax.broadcasted_iota(jnp.int32, (sub, tg), 1)

        def scatter(t, carry):
            lo = t * tg
            yt = y_s[tile_rows(t, tg), :].astype(BF16)
            for s in range(n_sub):
                @pl.when(overlaps(s, lo))
                def _():
                    rs = slice(s * sub, (s + 1) * sub)
                    slot = jnp.sum(jnp.where(lane == e, rcol_ref[rs, :], 0), axis=-1, keepdims=True)
                    weight = jnp.sum(jnp.where(lane == e, comb_ref[rs, :], 0.0), axis=-1, keepdims=True)
                    onehot = jnp.where(slot - lo == col, 1.0, 0.0).astype(BF16)
                    o_ref[rs, :] += weight * _dot(onehot, yt)
            return carry

        lax.fori_loop(0, n_moves, scatter, 0)

    @pl.when((e == pl.num_programs(1) - 1) & (j == pl.num_programs(2) - 1))
    def _():
        def finish(r, carry):
            rs = pl.ds(pl.multiple_of(r * sub, sub), sub)
            o_ref[rs, :] = _norm_then_ple(ALPHA * x_ref[rs, :] + o_ref[rs, :], g_ref, b_ref, p_ref.at[0, rs, :],
                                          pwg_ref, pbg_ref, pwp_ref)
            return carry

        lax.fori_loop(0, n_sub, finish, 0)


def _moe(x, comb, rank, starts, wg, wu, wd, layer, g, b, p, p_layer, ple_wg, ple_bg, ple_wp, *, tb, tf=896):
    t, d = x.shape
    _, ne, _, ff = wg.shape
    pd = p.shape[2]
    nb = t // tb
    rank_rows = rank.reshape(nb, tb, LANES)[:, :, :ne].transpose(0, 2, 1)
    starts = starts[:, :, :ne].transpose(0, 2, 1).reshape(-1)
    once = pl.Buffered(1)
    grid_spec = pltpu.PrefetchScalarGridSpec(
        num_scalar_prefetch=1,
        grid=(nb, ne, ff // tf),
        in_specs=[pl.BlockSpec((tb, d), lambda i, e, j, c: (i, 0), pipeline_mode=once),
                  pl.BlockSpec((tb, LANES), lambda i, e, j, c: (i, 0), pipeline_mode=once),
                  pl.BlockSpec((tb, LANES), lambda i, e, j, c: (i, 0), pipeline_mode=once),
                  pl.BlockSpec((1, ne, tb), lambda i, e, j, c: (i, 0, 0), pipeline_mode=once),
                  pl.BlockSpec((1, 1, d, tf), lambda i, e, j, c: (layer, e, 0, j)),
                  pl.BlockSpec((1, 1, d, tf), lambda i, e, j, c: (layer, e, 0, j)),
                  pl.BlockSpec((1, 1, tf, d), lambda i, e, j, c: (layer, e, j, 0)),
                  pl.BlockSpec((1, d), lambda i, e, j, c: (0, 0)), pl.BlockSpec((1, d), lambda i, e, j, c: (0, 0)),
                  pl.BlockSpec((1, tb, pd), lambda i, e, j, c: (p_layer, i, 0), pipeline_mode=once),
                  pl.BlockSpec((d, d), lambda i, e, j, c: (0, 0), pipeline_mode=once),
                  pl.BlockSpec((1, d), lambda i, e, j, c: (0, 0)),
                  pl.BlockSpec((pd, d), lambda i, e, j, c: (0, 0), pipeline_mode=once)],
        out_specs=pl.BlockSpec((tb, d), lambda i, e, j, c: (i, 0), pipeline_mode=once),
        scratch_shapes=[pltpu.VMEM((tb, d), BF16), pltpu.VMEM((tb, d), BF16), pltpu.VMEM((tb, d), F32)],
    )
    return pl.pallas_call(
        _moe_body,
        grid_spec=grid_spec,
        out_shape=jax.ShapeDtypeStruct((t, d), F32),
        compiler_params=pltpu.CompilerParams(dimension_semantics=("parallel", "arbitrary", "arbitrary"),
                                             vmem_limit_bytes=MOE_VMEM_LIMIT_BYTES),
        name="moe",
    )(starts, x, comb, rank, rank_rows, wg, wu, wd, g.reshape(1, d), b.reshape(1, d), p, ple_wg, ple_bg.reshape(1, d),
      ple_wp)


def _moe_layer(h, w_router, wg, wu, wd, layer, g, b, p, p_layer, ple_wg, ple_bg, ple_wp, *, tb=2048):
    tb = min(tb, h.shape[0])
    comb, rank, starts = _router(h, w_router, tb=tb)
    return _moe(h, comb, rank, starts, wg, wu, wd, layer, g, b, p, p_layer, ple_wg, ple_bg, ple_wp, tb=tb)


def _gmlp_gate_body(u_ref, v_ref, lng_ref, lnb_ref, ws_ref, bs_ref, wo_ref, h_ref, g_ref, b_ref, o_ref, gated_ref):
    tm, width = u_ref.shape
    gw = width // GM_GROUPS
    v = _layer_norm(v_ref[...].astype(F32), lng_ref[...], lnb_ref[...]).astype(BF16)
    mask = _tril(GM_CHUNK)
    for g in range(GM_GROUPS):
        ws = jnp.where(mask, ws_ref[g], 0.0).astype(BF16)
        bias = bs_ref[:, g:g + 1]
        for c in range(tm // GM_CHUNK):
            rows = slice(c * GM_CHUNK, (c + 1) * GM_CHUNK)
            cols = slice(g * gw, (g + 1) * gw)
            sv = _dot(ws, v[rows, cols]) + bias
            gated_ref[rows, cols] = (u_ref[rows, cols].astype(F32) * sv).astype(BF16)
    acc = _dot(gated_ref[...], wo_ref[...])
    o_ref[...] = _layer_norm(ALPHA * h_ref[...] + acc, g_ref[...], b_ref[...])


def _gmlp_mixer(h, w_in, b_in, ln_g, ln_b, w_s, b_s, w_out, g, b, *, tm=512):
    t, d = h.shape
    width = w_out.shape[0]
    tm = min(tm, t)
    z = _linear_gelu(h, w_in.astype(BF16), b_in)
    row = lambda i: (i, 0)
    const = lambda i: (0, 0)
    return pl.pallas_call(
        _gmlp_gate_body,
        grid=(t // tm,),
        in_specs=[pl.BlockSpec((tm, width), row), pl.BlockSpec((tm, width), lambda i: (i, 1)),
                  pl.BlockSpec((1, width), const), pl.BlockSpec((1, width), const),
                  pl.BlockSpec((GM_GROUPS, GM_CHUNK, GM_CHUNK), lambda i: (0, 0, 0)),
                  pl.BlockSpec((GM_CHUNK, GM_GROUPS), const), pl.BlockSpec((width, d), const),
                  pl.BlockSpec((tm, d), row), pl.BlockSpec((1, d), const), pl.BlockSpec((1, d), const)],
        out_specs=pl.BlockSpec((tm, d), row),
        out_shape=jax.ShapeDtypeStruct((t, d), F32),
        scratch_shapes=[pltpu.VMEM((tm, width), BF16)],
        compiler_params=_params("parallel"),
        name="gmlp_gate",
    )(z, z, ln_g.reshape(1, width), ln_b.reshape(1, width), w_s, b_s.T, w_out.astype(BF16), h,
      g.reshape(1, d), b.reshape(1, d))


def _fox_decay_body(f_ref, bf_ref, c_ref):
    s = f_ref.shape[1]
    x = f_ref[0] + bf_ref[...]
    log_f = jnp.minimum(x, 0.0) - jnp.log1p(jnp.exp(-jnp.abs(x)))
    tri = _tril(LANES).astype(F32)
    carry = jnp.zeros((1, LANES), F32)
    lane = lax.broadcasted_iota(jnp.int32, (LANES, LANES), 1)
    hds = FOX_HEADS
    for i in range(s // LANES):
        cs = _dot_hi(tri, log_f[i * LANES:(i + 1) * LANES]) + carry
        carry = cs[LANES - 1:LANES]
        hi = cs.astype(BF16).astype(F32)
        mid = (cs - hi).astype(BF16).astype(F32)
        lo = (cs - hi - mid).astype(BF16).astype(F32)
        out = jnp.where(lane < hds, hi, 0.0)
        for slot, piece in enumerate((mid, lo, -hi, -mid, -lo), start=1):
            moved = pltpu.roll(piece, slot * hds, 1)
            out = jnp.where((lane >= slot * hds) & (lane < (slot + 1) * hds), moved, out)
        c_ref[0, i * LANES:(i + 1) * LANES, :] = out.astype(c_ref.dtype)


def _fox_decay(f, b_f):
    bsz, s, _ = f.shape
    blk = lambda i: (i, 0, 0)
    return pl.pallas_call(
        _fox_decay_body,
        grid=(bsz,),
        in_specs=[pl.BlockSpec((1, s, LANES), blk), pl.BlockSpec((1, LANES), lambda i: (0, 0))],
        out_specs=pl.BlockSpec((1, s, LANES), blk),
        out_shape=jax.ShapeDtypeStruct((bsz, s, LANES), BF16),
        compiler_params=_params("parallel"),
        name="fox_decay",
    )(f, b_f)


FOX_BIAS_LANES = 6


def _fox_attn_body(q_ref, k_ref, v_ref, cq_ref, ck_ref, o_ref, k_s, v_s):
    grp, qi = pl.program_id(1), pl.program_id(2)
    tq = q_ref.shape[1]
    half = FOX_HEAD_DIM
    pieces = FOX_BIAS_LANES // 2
    heads = range(FOX_HEADS_PER_STEP)
    lane = lax.broadcasted_iota(jnp.int32, (1, LANES), 1)
    src = lax.broadcasted_iota(jnp.int32, (LANES, LANES), 0)
    dst = lax.broadcasted_iota(jnp.int32, (LANES, LANES), 1)
    scale = jnp.where(lane < half, half ** -0.5, 1.0).astype(BF16)

    def head_features(ref, hh):
        pair = ref[0, :, (hh // 2) * LANES:(hh // 2 + 1) * LANES]
        sel = jnp.where((dst < half) & (src == dst + half * (hh % 2)), 1.0, 0.0).astype(BF16)
        return _dot(pair, sel)

    def bias_lanes(c6, hh, first_piece, at, ones_at):
        head = grp * FOX_HEADS_PER_STEP + hh
        piece = dst - at + first_piece
        sel = jnp.where((dst >= at) & (dst < at + pieces) & (src == piece * FOX_HEADS + head), 1.0, 0.0).astype(BF16)
        return _dot(c6, sel) + jnp.where((lane >= ones_at) & (lane < ones_at + pieces), 1.0, 0.0)

    @pl.when(qi == 0)
    def _():
        ck = ck_ref[0]
        for hh in heads:
            k_s[hh] = (head_features(k_ref, hh) + bias_lanes(ck, hh, pieces, half + pieces, half)).astype(BF16)
            v_aug = head_features(v_ref, hh) + jnp.where(lane >= half, 1.0, 0.0)
            for j in range(v_s.shape[1]):
                v_s[hh, j] = v_aug[j * tq:(j + 1) * tq].T.astype(BF16)

    cq = cq_ref[0]
    qs = [(head_features(q_ref, hh) + bias_lanes(cq, hh, 0, half, half + pieces)).astype(BF16) * scale for hh in heads]
    kv_row = lax.broadcasted_iota(jnp.int32, (tq, tq), 0)
    q_col = lax.broadcasted_iota(jnp.int32, (tq, tq), 1)
    diag = kv_row <= q_col

    def scores(kj):
        rows = pl.ds(pl.multiple_of(kj * tq, tq), tq)
        return tuple(_dot_nt(k_s[hh, rows, :], qs[hh]) for hh in heads)

    def accumulate(kj, ss, carry, masked):
        if masked:
            ss = [jnp.where(diag, s, -1e30) for s in ss]
        ms = [jnp.maximum(carry[hh][0], jnp.max(ss[hh], axis=0, keepdims=True)) for hh in heads]
        ps = [jnp.exp(ss[hh] - ms[hh]).astype(BF16) for hh in heads]
        pvs = [_dot(v_s[hh, kj], ps[hh]) for hh in heads]
        return tuple((ms[hh], jnp.exp(carry[hh][0] - ms[hh]) * carry[hh][1] + pvs[hh]) for hh in heads)

    def block(kj, state):
        ss, carry = state
        return scores(kj + 1), accumulate(kj, ss, carry, masked=False)

    init = tuple((jnp.full((1, tq), -1e30, F32), jnp.zeros((LANES, tq), F32)) for _ in heads)
    ss, carry = lax.fori_loop(0, qi, block, (scores(0), init))
    carry = accumulate(qi, ss, carry, masked=True)
    outs = [(acc / acc[half:half + 1, :]).T for _, acc in carry]
    for pr in range(len(outs) // 2):
        pair = jnp.where(lane < half, outs[2 * pr], pltpu.roll(outs[2 * pr + 1], half, 1))
        o_ref[0, :, pr * LANES:(pr + 1) * LANES] = pair.astype(o_ref.dtype)


FOX_HEADS_PER_STEP = 4


def _fox_attention(qkv, c6, *, tq=256):
    bsz, s, w3 = qkv.shape
    tq = min(tq, s)
    hps = FOX_HEADS_PER_STEP
    gw = hps * FOX_HEAD_DIM
    groups = w3 // (3 * gw)
    return pl.pallas_call(
        _fox_attn_body,
        grid=(bsz, groups, s // tq),
        in_specs=[pl.BlockSpec((1, tq, gw), lambda b, h, i: (b, i, h)),
                  pl.BlockSpec((1, s, gw), lambda b, h, i: (b, 0, groups + h)),
                  pl.BlockSpec((1, s, gw), lambda b, h, i: (b, 0, 2 * groups + h)),
                  pl.BlockSpec((1, tq, LANES), lambda b, h, i: (b, i, 0)),
                  pl.BlockSpec((1, s, LANES), lambda b, h, i: (b, 0, 0))],
        out_specs=pl.BlockSpec((1, tq, gw), lambda b, h, i: (b, i, h)),
        out_shape=jax.ShapeDtypeStruct((bsz, s, w3 // 3), BF16),
        scratch_shapes=[pltpu.VMEM((hps, s, LANES), BF16), pltpu.VMEM((hps, s // tq, LANES, tq), BF16)],
        compiler_params=_params("parallel", "parallel", "arbitrary"),
        name="fox_attn",
    )(qkv, qkv, qkv, c6, c6)


def _fox_mixer(h, bsz, w_in, b_f, w_out, g, b):
    t, d = h.shape
    s = t // bsz
    hds, dh = FOX_HEADS, FOX_HEAD_DIM
    width = hds * dh
    qkv = _linear(h, w_in[:, :3 * width].astype(BF16))
    w_f = jnp.pad(w_in[:, 3 * width:], ((0, 0), (0, LANES - hds)))
    f = _linear(h, w_f, hi=True, out_dtype=F32, tn=LANES)
    b_pad = jnp.pad(b_f, (0, LANES - hds)).reshape(1, LANES)
    c6 = _fox_decay(f.reshape(bsz, s, LANES), b_pad)
    o = _fox_attention(qkv.reshape(bsz, s, 3 * width), c6)
    return _proj_ln(o.reshape(t, width), w_out.astype(BF16), h, g, b)


def _causal_conv(x, taps):
    k = len(taps)
    row = lax.broadcasted_iota(jnp.int32, x.shape, 0)
    out = x * taps[k - 1]
    for sh in range(1, k):
        shifted = jnp.where(row >= sh, pltpu.roll(x, sh, 0), 0.0)
        out = out + shifted * taps[k - 1 - sh]
    return out


def _gdn_gates_body(x_ref, dtb_ref, alog_ref, o_ref):
    x = x_ref[...]
    lane = lax.broadcasted_iota(jnp.int32, x.shape, 1)
    beta = _sigmoid(x)
    gdec = -jnp.exp(alog_ref[...]) * _softplus(x + dtb_ref[...])
    gdec = jnp.where((lane >= GDN_HEADS) & (lane < 2 * GDN_HEADS), gdec, 0.0)
    tri = _tril(GDN_CHUNK).astype(F32)
    lane_c = lax.broadcasted_iota(jnp.int32, (GDN_CHUNK, LANES), 1)
    for c in range(x.shape[0] // GDN_CHUNK):
        rows = slice(c * GDN_CHUNK, (c + 1) * GDN_CHUNK)
        gc = _dot_hi(tri, gdec[rows])
        o_ref[rows, :] = jnp.where(lane_c < GDN_HEADS, beta[rows], gc)


def _gdn_gates(x, dt_bias, a_log, *, tm=512):
    t = x.shape[0]
    tm = min(tm, t)
    pad = lambda v: jnp.pad(v, (GDN_HEADS, LANES - 2 * GDN_HEADS)).reshape(1, LANES)
    return pl.pallas_call(
        _gdn_gates_body,
        grid=(t // tm,),
        in_specs=[pl.BlockSpec((tm, LANES), lambda i: (i, 0)), pl.BlockSpec((1, LANES), lambda i: (0, 0)),
                  pl.BlockSpec((1, LANES), lambda i: (0, 0))],
        out_specs=pl.BlockSpec((tm, LANES), lambda i: (i, 0)),
        out_shape=jax.ShapeDtypeStruct((t, LANES), F32),
        compiler_params=_params("parallel"),
        name="gdn_gates",
    )(x, pad(dt_bias), pad(a_log))


def _unit_lower_inverses(ms):
    n = ms[0].shape[0]
    eye = (lax.broadcasted_iota(jnp.int32, (n, n), 0) == lax.broadcasted_iota(jnp.int32, (n, n), 1)).astype(F32)
    ps = [-m for m in ms]
    invs = [eye + p for p in ps]
    step = 1
    while 2 * step < n:
        ps = [_dot_split(p, p) for p in ps]
        invs = [inv + _dot_split(inv, p) for inv, p in zip(invs, ps)]
        step *= 2
    return invs


GDN_HEADS_PER_STEP = 2
GDN_CHUNKS_PER_ITER = 4


def _gdn_body(q_ref, k_ref, v_ref, gate_ref, cw_ref, gcol_ref, grow_ref, ng_ref, o_ref,
              q_s, k_s, kb_s, uw_s, g_s, a_s, o_s):
    hb = GDN_HEADS_PER_STEP
    grp = GDN_CHUNKS_PER_ITER
    hd0 = pl.program_id(1) * hb
    s = q_ref.shape[1]
    c = GDN_CHUNK
    dk = GDN_DK
    n_chunks = s // c

    gcols = gcol_ref[0]
    lane = lax.broadcasted_iota(jnp.int32, gcols.shape, 1)
    for hh in range(hb):
        cols = slice(hh * dk, (hh + 1) * dk)

        def conv_act(x_ref, which):
            taps = [cw_ref[hh, which, i:i + 1, :] for i in range(cw_ref.shape[2])]
            return _silu(_causal_conv(x_ref[0, :, cols].astype(F32), taps))

        q = conv_act(q_ref, 0)
        k = conv_act(k_ref, 1)
        v = conv_act(v_ref, 2)
        q = q * lax.rsqrt(jnp.sum(q * q, axis=-1, keepdims=True) + L2_EPS) * (dk ** -0.5)
        k = k * lax.rsqrt(jnp.sum(k * k, axis=-1, keepdims=True) + L2_EPS)
        beta = jnp.sum(jnp.where(lane == hd0 + hh, gcols, 0.0), axis=-1, keepdims=True)
        gc = jnp.sum(jnp.where(lane == hd0 + hh + GDN_HEADS, gcols, 0.0), axis=-1, keepdims=True)
        kb = k * beta
        q_s[hh] = q
        k_s[hh] = k
        kb_s[hh] = kb
        uw_s[hh, :, 0:dk] = v * beta
        uw_s[hh, :, dk:2 * dk] = kb * jnp.exp(gc)
        g_s[hh] = jnp.broadcast_to(gc, g_s.shape[1:])
    causal = _tril(c)
    strict = _tril(c, strict=True)

    def prep(it, carry):
        base = pl.multiple_of(it * (grp * c), grp * c)
        where, ms, aqks = [], [], []
        for hh in range(hb):
            for gi in range(grp):
                rows = pl.ds(base + gi * c, c)
                kc = k_s[hh, rows, :]
                gcc = g_s[hh, rows, 0:1]
                grow = grow_ref[0, hh, pl.ds(it * grp + gi, 1), :]
                decay = jnp.exp(jnp.where(causal, gcc - grow, -jnp.inf))
                ms.append(jnp.where(strict, _dot_nt(kb_s[hh, rows, :], kc) * decay, 0.0))
                aqks.append(_dot_nt(q_s[hh, rows, :], kc) * decay)
                where.append((hh, gi, rows))
        invs = _unit_lower_inverses(ms)
        uws = [_dot_split(inv, uw_s[hh, rows, :]) for inv, (hh, gi, rows) in zip(invs, where)]
        for (hh, gi, rows), aqk, uw in zip(where, aqks, uws):
            a_s[hh, it * grp + gi] = aqk
            uw_s[hh, rows, :] = uw
            gcc = g_s[hh, rows, 0:1]
            g_last = g_s[hh, pl.ds(base + gi * c + c - 1, 1), 0:1]
            q_s[hh, rows, :] = q_s[hh, rows, :] * jnp.exp(gcc)
            k_s[hh, rows, :] = k_s[hh, rows, :] * jnp.exp(g_last - gcc)
        return carry

    lax.fori_loop(0, n_chunks // grp, prep, 0)

    def scan(i, states):
        rows = pl.ds(pl.multiple_of(i * c, c), c)
        heads = range(hb)
        v_news = [uw_s[hh, rows, 0:dk] - _dot(uw_s[hh, rows, dk:2 * dk], states[hh]) for hh in heads]
        for hh in heads:
            o_s[hh, rows, :] = _dot(q_s[hh, rows, :], states[hh]) + _dot(a_s[hh, i], v_news[hh])
        decays = [jnp.exp(g_s[hh, pl.ds(i * c + c - 1, 1), 0:1]) for hh in heads]
        return tuple(states[hh] * decays[hh] + _dot_tn(k_s[hh, rows, :], v_news[hh]) for hh in heads)

    lax.fori_loop(0, n_chunks, scan, tuple(jnp.zeros((dk, dk), F32) for _ in range(hb)))
    for hh in range(hb):
        cols = slice(hh * dk, (hh + 1) * dk)
        o = o_s[hh]
        o = o * lax.rsqrt(jnp.mean(o * o, axis=-1, keepdims=True) + RMS_EPS) * ng_ref[...]
        o_ref[0, :, cols] = (o * _silu(gate_ref[0, :, cols].astype(F32))).astype(o_ref.dtype)


def _gdn_core(proj, conv_w, gcol, grow, norm_g):
    bsz, s, _ = proj.shape
    hd, dk, c, hb = GDN_HEADS, GDN_DK, GDN_CHUNK, GDN_HEADS_PER_STEP
    n = s // c
    steps = hd // hb
    cw = conv_w.reshape(conv_w.shape[0], 3, hd, dk).transpose(2, 1, 0, 3)
    kk = conv_w.shape[0]
    seq = lambda off: pl.BlockSpec((1, s, hb * dk), lambda b, h: (b, 0, off + h))
    per_head = lambda *shape: pltpu.VMEM((hb,) + shape, F32)
    return pl.pallas_call(
        _gdn_body,
        grid=(bsz, steps),
        in_specs=[seq(0), seq(steps), seq(2 * steps), seq(3 * steps),
                  pl.BlockSpec((hb, 3, kk, dk), lambda b, h: (h, 0, 0, 0)),
                  pl.BlockSpec((1, s, LANES), lambda b, h: (b, 0, 0)),
                  pl.BlockSpec((1, hb, n, c), lambda b, h: (b, h, 0, 0)),
                  pl.BlockSpec((1, dk), lambda b, h: (0, 0))],
        out_specs=seq(0),
        out_shape=jax.ShapeDtypeStruct((bsz, s, hd * dk), BF16),
        scratch_shapes=[per_head(s, dk), per_head(s, dk), per_head(s, dk), per_head(s, 2 * dk), per_head(s, LANES),
                        per_head(n, c, c), per_head(s, dk)],
        compiler_params=_params("parallel", "arbitrary"),
        name="gdn_core",
    )(proj, proj, proj, proj, cw, gcol, grow, norm_g.reshape(1, dk))


def _gdn_mixer(h, bsz, w_in, conv_w, a_log, dt_bias, norm_g, w_out, g, b):
    t, d = h.shape
    s = t // bsz
    hd = GDN_HEADS
    qkv_w = 3 * hd * GDN_DK
    gate_w = hd * GDN_DK
    proj = _linear(h, w_in[:, :qkv_w + gate_w].astype(BF16))
    w_small = jnp.pad(w_in[:, qkv_w + gate_w:], ((0, 0), (0, LANES - 2 * hd)))
    logits = _linear(h, w_small, hi=True, out_dtype=F32, tn=LANES)
    gcol = _gdn_gates(logits, dt_bias, a_log).reshape(bsz, s, LANES)
    grow = gcol[:, :, hd:2 * hd].transpose(0, 2, 1).reshape(bsz, hd, s // GDN_CHUNK, GDN_CHUNK)
    o = _gdn_core(proj.reshape(bsz, s, qkv_w + gate_w), conv_w, gcol, grow, norm_g)
    return _proj_ln(o.reshape(t, gate_w), w_out.astype(BF16), h, g, b)


def _ssd_gates_body(x_ref, dtb_ref, alog_ref, o_ref):
    x = x_ref[...]
    lane = lax.broadcasted_iota(jnp.int32, x.shape, 1)
    dt = _softplus(x + dtb_ref[...])
    a = jnp.where(lane < SSD_HEADS, dt * (-jnp.exp(alog_ref[...])), 0.0)
    tri = _tril(SSD_CHUNK).astype(F32)
    lane_c = lax.broadcasted_iota(jnp.int32, (SSD_CHUNK, LANES), 1)
    for c in range(x.shape[0] // SSD_CHUNK):
        rows = slice(c * SSD_CHUNK, (c + 1) * SSD_CHUNK)
        acum = pltpu.roll(_dot_hi(tri, a[rows]), SSD_HEADS, 1)
        o_ref[rows, :] = jnp.where(lane_c < SSD_HEADS, dt[rows], acum)


def _ssd_gates(x, dt_bias, a_log, *, tm=512):
    t = x.shape[0]
    tm = min(tm, t)
    pad = lambda v: jnp.pad(v, (0, LANES - SSD_HEADS)).reshape(1, LANES)
    return pl.pallas_call(
        _ssd_gates_body,
        grid=(t // tm,),
        in_specs=[pl.BlockSpec((tm, LANES), lambda i: (i, 0)), pl.BlockSpec((1, LANES), lambda i: (0, 0)),
                  pl.BlockSpec((1, LANES), lambda i: (0, 0))],
        out_specs=pl.BlockSpec((tm, LANES), lambda i: (i, 0)),
        out_shape=jax.ShapeDtypeStruct((t, LANES), F32),
        compiler_params=_params("parallel"),
        name="ssd_gates",
    )(x, pad(dt_bias), pad(a_log))


def _ssd_body(x_ref, b_ref, c_ref, z_ref, cwx_ref, cwb_ref, cwc_ref, cbx_ref, cbb_ref, cbc_ref,
              gcol_ref, arow_ref, d_ref, ng_ref, o_ref, x_s, b_s, c_s, y_s):
    grp = pl.program_id(1)
    s = x_ref.shape[1]
    c = SSD_CHUNK
    hp = SSD_HEAD_DIM
    heads = x_ref.shape[2] // hp
    n_chunks = s // c

    def conv_act(ref, w_ref, bias_ref):
        taps = [w_ref[i:i + 1, :] for i in range(w_ref.shape[0])]
        return _silu(_causal_conv(ref[0].astype(F32), taps) + bias_ref[...])

    x_s[...] = conv_act(x_ref, cwx_ref, cbx_ref)
    b_s[...] = conv_act(b_ref, cwb_ref, cbb_ref)
    c_s[...] = conv_act(c_ref, cwc_ref, cbc_ref)
    causal = _tril(c)
    pair_lane = lax.broadcasted_iota(jnp.int32, (c, 2 * hp), 1) < hp

    def head_cols(gcol, base):
        lane = lax.broadcasted_iota(jnp.int32, gcol.shape, 1)
        return [jnp.sum(jnp.where(lane == base + grp * heads + j, gcol, 0.0), axis=-1, keepdims=True)
                for j in range(heads)]

    def widen(cols):
        pairs = [jnp.where(pair_lane, cols[2 * p], cols[2 * p + 1]) for p in range(heads // 2)]
        return jnp.concatenate(pairs, axis=1)

    def step(i, state):
        rows = pl.ds(pl.multiple_of(i * c, c), c)
        gcol = gcol_ref[0, rows, :]
        dt = head_cols(gcol, 0)
        acum = head_cols(gcol, SSD_HEADS)
        arow = arow_ref[0, 0, i]
        xc = x_s[rows, :]
        bc = b_s[rows, :]
        cc = c_s[rows, :]
        xdt = xc * widen(dt)
        cb = _dot_nt(cc, bc)
        y = _dot(cc, state) * widen([jnp.exp(a) for a in acum])
        intra = []
        for p in range(heads // 2):
            xp = xdt[:, 2 * p * hp:(2 * p + 2) * hp]
            ys = []
            for j in (2 * p, 2 * p + 1):
                lmat = jnp.exp(jnp.where(causal, acum[j] - arow[j:j + 1, :], -jnp.inf))
                ys.append(_dot(cb * lmat, xp))
            intra.append(jnp.where(pair_lane, ys[0], ys[1]))
        y = y + jnp.concatenate(intra, axis=1) + xc * d_ref[...]
        y_s[rows, :] = y
        last = head_cols(gcol_ref[0, pl.ds(i * c + c - 1, 1), :], SSD_HEADS)
        xdec = xdt * widen([jnp.exp(last[j] - acum[j]) for j in range(heads)])
        first_lane = lax.broadcasted_iota(jnp.int32, (1, 2 * hp), 1) < hp
        lastw = jnp.concatenate([jnp.where(first_lane, jnp.exp(last[2 * p]), jnp.exp(last[2 * p + 1]))
                                 for p in range(heads // 2)], axis=1)
        return state * lastw + _dot_tn(bc, xdec)

    lax.fori_loop(0, n_chunks, step, jnp.zeros((SSD_STATE, x_ref.shape[2]), F32))
    y = y_s[...] * _silu(z_ref[0].astype(F32))
    y = y * lax.rsqrt(jnp.mean(y * y, axis=-1, keepdims=True) + RMS_EPS) * ng_ref[...]
    o_ref[0] = y.astype(o_ref.dtype)


def _ssd_core(proj, conv_w, conv_b, gcol, arow, d_wide, norm_g):
    bsz, s, _ = proj.shape
    grp, n_state, c = SSD_GROUPS, SSD_STATE, SSD_CHUNK
    inner = SSD_HEADS * SSD_HEAD_DIM
    gw = inner // grp
    heads = SSD_HEADS // grp
    n = s // c
    kk = conv_w.shape[0]
    xblk = lambda off: pl.BlockSpec((1, s, gw), lambda b, g: (b, 0, off + g))
    nblk = lambda off: pl.BlockSpec((1, s, n_state), lambda b, g: (b, 0, off + g))
    wx = lambda rows, off: pl.BlockSpec((rows, gw), lambda b, g: (0, off + g))
    wn = lambda rows, off: pl.BlockSpec((rows, n_state), lambda b, g: (0, off + g))
    x_off = inner // gw
    b_off = 2 * inner // n_state
    c_off = b_off + grp
    return pl.pallas_call(
        _ssd_body,
        grid=(bsz, grp),
        in_specs=[xblk(x_off), nblk(b_off), nblk(c_off), xblk(0),
                  wx(kk, 0), wn(kk, inner // n_state), wn(kk, inner // n_state + grp),
                  wx(1, 0), wn(1, inner // n_state), wn(1, inner // n_state + grp),
                  pl.BlockSpec((1, s, LANES), lambda b, g: (b, 0, 0)),
                  pl.BlockSpec((1, 1, n, heads, c), lambda b, g: (b, g, 0, 0, 0)),
                  wx(1, 0), wx(1, 0)],
        out_specs=xblk(0),
        out_shape=jax.ShapeDtypeStruct((bsz, s, inner), BF16),
        scratch_shapes=[pltpu.VMEM((s, gw), F32), pltpu.VMEM((s, n_state), F32), pltpu.VMEM((s, n_state), F32),
                        pltpu.VMEM((s, gw), F32)],
        compiler_params=_params("parallel", "arbitrary"),
        name="ssd_core",
    )(proj, proj, proj, proj, conv_w, conv_w, conv_w, conv_b, conv_b, conv_b, gcol, arow, d_wide, norm_g)


def _ssd_mixer(h, bsz, w_in, conv_w, conv_b, dt_bias, a_log, d_skip, norm_g, w_out, g, b):
    t, d = h.shape
    s = t // bsz
    inner = SSD_HEADS * SSD_HEAD_DIM
    main = 2 * inner + 2 * SSD_GROUPS * SSD_STATE
    heads = SSD_HEADS // SSD_GROUPS
    proj = _linear(h, w_in[:, :main].astype(BF16))
    w_dt = jnp.pad(w_in[:, main:], ((0, 0), (0, LANES - SSD_HEADS)))
    dt_logits = _linear(h, w_dt, hi=True, out_dtype=F32, tn=LANES)
    gcol = _ssd_gates(dt_logits, dt_bias, a_log).reshape(bsz, s, LANES)
    arow = gcol[:, :, SSD_HEADS:2 * SSD_HEADS].reshape(bsz, s // SSD_CHUNK, SSD_CHUNK, SSD_GROUPS, heads)
    arow = arow.transpose(0, 3, 1, 4, 2)
    d_wide = jnp.repeat(d_skip, SSD_HEAD_DIM).reshape(1, inner)
    o = _ssd_core(proj.reshape(bsz, s, main), conv_w, conv_b.reshape(1, -1), gcol, arow, d_wide,
                  norm_g.reshape(1, inner))
    return _proj_ln(o.reshape(t, inner), w_out.astype(BF16), h, g, b)


def kernel(x, p, ln_g, ln_b, gm_w_in, gm_b_in, gm_ln_g, gm_ln_b, gm_w_s, gm_b_s, gm_w_out, fox_w_in, fox_b_f, fox_w_out, gdn_w_in, gdn_conv_w, gdn_a_log, gdn_dt_bias, gdn_norm_g, gdn_w_out, ssd_w_in, ssd_conv_w, ssd_conv_b, ssd_dt_bias, ssd_a_log, ssd_d, ssd_norm_g, ssd_w_out, ffn_w_gate, ffn_w_up, ffn_w_down, moe_w_router, moe_w_gate, moe_w_up, moe_w_down, ple_w_gate, ple_b_gate, ple_w_proj):
    bsz, s, d = x.shape
    t = bsz * s
    h = x.reshape(t, d)
    pf = p.reshape(p.shape[0], t, p.shape[-1])
    ffn_w = tuple(w.astype(BF16) for w in (ffn_w_gate, ffn_w_up, ffn_w_down))
    moe_w = tuple(w.astype(BF16) for w in (moe_w_gate, moe_w_up, moe_w_down))
    for i in range(DEPTH):
        m, j = i % 4, i // 4
        g0, b0 = ln_g[i, 0], ln_b[i, 0]
        if m == 0:
            h = _gmlp_mixer(h, gm_w_in[j], gm_b_in[j], gm_ln_g[j], gm_ln_b[j], gm_w_s[j], gm_b_s[j], gm_w_out[j], g0, b0)
        elif m == 1:
            h = _fox_mixer(h, bsz, fox_w_in[j], fox_b_f[j], fox_w_out[j], g0, b0)
        elif m == 2:
            h = _gdn_mixer(h, bsz, gdn_w_in[j], gdn_conv_w[j], gdn_a_log[j], gdn_dt_bias[j], gdn_norm_g[j], gdn_w_out[j],
                           g0, b0)
        else:
            h = _ssd_mixer(h, bsz, ssd_w_in[j], ssd_conv_w[j], ssd_conv_b[j], ssd_dt_bias[j], ssd_a_log[j], ssd_d[j],
                           ssd_norm_g[j], ssd_w_out[j], g0, b0)
        g1, b1 = ln_g[i, 1], ln_b[i, 1]
        ple = (pf, i, ple_w_gate[i].astype(BF16), ple_b_gate[i], ple_w_proj[i].astype(BF16))
        if i % 2 == 0:
            h = _ffn(h, *ffn_w, i // 2, g1, b1, *ple)
        else:
            h = _moe_layer(h, moe_w_router[i // 2], *moe_w, i // 2, g1, b1, *ple)
    return h.reshape(bsz, s, d)
```

```python
import functools
import math

import jax
import jax.numpy as jnp
from jax import lax
from jax.experimental import pallas as pl
from jax.experimental.pallas import tpu as pltpu

F32 = jnp.float32
BF16 = jnp.bfloat16
HIGHEST = lax.Precision.HIGHEST

DEPTH = 4
ALPHA = (2 * DEPTH) ** 0.25
LN_EPS = 1e-5
RMS_EPS = 1e-6
L2_EPS = 1e-6

GM_CHUNK = 128
GM_GROUPS = 8
FOX_HEADS = 16
FOX_HEAD_DIM = 64
GDN_HEADS = 8
GDN_DK = 128
GDN_CHUNK = 64
SSD_HEADS = 32
SSD_HEAD_DIM = 64
SSD_GROUPS = 4
SSD_STATE = 128
SSD_CHUNK = 64
N_EXPERTS = 8
LANES = 128

VMEM_LIMIT_BYTES = 56 * 1024 * 1024
MOE_VMEM_LIMIT_BYTES = 60 * 1024 * 1024


def _params(*sem):
    return pltpu.CompilerParams(dimension_semantics=sem, vmem_limit_bytes=VMEM_LIMIT_BYTES)


def _dot(a, b):
    return jnp.dot(a, b, preferred_element_type=F32)


def _dot_nt(a, b):
    return lax.dot_general(a, b, (((1,), (1,)), ((), ())), preferred_element_type=F32)


def _dot_tn(a, b):
    return lax.dot_general(a, b, (((0,), (0,)), ((), ())), preferred_element_type=F32)


def _dot_hi(a, b):
    return jnp.dot(a, b, preferred_element_type=F32, precision=HIGHEST)


def _split_bf16(a):
    hi = a.astype(BF16)
    return hi, (a - hi.astype(F32)).astype(BF16)


def _dot_split(a, b):
    ah, al = _split_bf16(a)
    bh, bl = _split_bf16(b)
    return _dot(ah, bh) + (_dot(ah, bl) + _dot(al, bh))


def _layer_norm(r, g, b):
    mu = jnp.mean(r, axis=-1, keepdims=True)
    d = r - mu
    var = jnp.mean(d * d, axis=-1, keepdims=True)
    return d * lax.rsqrt(var + LN_EPS) * g + b


def _sigmoid(x):
    return 1.0 / (1.0 + jnp.exp(-x))


def _silu(x):
    return x * _sigmoid(x)


def _softplus(x):
    return jnp.maximum(x, 0.0) + jnp.log1p(jnp.exp(-jnp.abs(x)))


def _tril(n, strict=False):
    r = lax.broadcasted_iota(jnp.int32, (n, n), 0)
    c = lax.broadcasted_iota(jnp.int32, (n, n), 1)
    return (r > c) if strict else (r >= c)


def _linear_body(x_ref, w_ref, o_ref, *, hi):
    if hi:
        y = _dot_split(x_ref[...], w_ref[...])
    else:
        y = _dot(x_ref[...].astype(BF16), w_ref[...])
    o_ref[...] = y.astype(o_ref.dtype)


def _linear(x, w, *, hi=False, out_dtype=BF16, tm=1024, tn=1024):
    t, k = x.shape
    n = w.shape[1]
    tm, tn = min(tm, t), min(tn, n)
    return pl.pallas_call(
        functools.partial(_linear_body, hi=hi),
        grid=(t // tm, n // tn),
        in_specs=[pl.BlockSpec((tm, k), lambda i, j: (i, 0)), pl.BlockSpec((k, tn), lambda i, j: (0, j))],
        out_specs=pl.BlockSpec((tm, tn), lambda i, j: (i, j)),
        out_shape=jax.ShapeDtypeStruct((t, n), out_dtype),
        compiler_params=_params("parallel", "arbitrary"),
        name="linear_hi" if hi else "linear",
    )(x, w)


def _linear_gelu_body(x_ref, w_ref, b_ref, o_ref):
    y = _dot(x_ref[...].astype(BF16), w_ref[...]) + b_ref[...]
    o_ref[...] = jax.nn.gelu(y).astype(o_ref.dtype)


def _linear_gelu(x, w, b, *, tm=1024, tn=1024):
    t, k = x.shape
    n = w.shape[1]
    tm = min(tm, t)
    return pl.pallas_call(
        _linear_gelu_body,
        grid=(t // tm, n // tn),
        in_specs=[pl.BlockSpec((tm, k), lambda i, j: (i, 0)), pl.BlockSpec((k, tn), lambda i, j: (0, j)),
                  pl.BlockSpec((1, tn), lambda i, j: (0, j))],
        out_specs=pl.BlockSpec((tm, tn), lambda i, j: (i, j)),
        out_shape=jax.ShapeDtypeStruct((t, n), BF16),
        compiler_params=_params("parallel", "arbitrary"),
        name="linear_gelu",
    )(x, w, b.reshape(1, n))


def _proj_ln_body(y_ref, w_ref, h_ref, g_ref, b_ref, o_ref):
    acc = _dot(y_ref[...].astype(BF16), w_ref[...])
    o_ref[...] = _layer_norm(ALPHA * h_ref[...] + acc, g_ref[...], b_ref[...])


def _proj_ln(y, w, h, g, b, *, tm=512):
    t, k = y.shape
    d = w.shape[1]
    tm = min(tm, t)
    row = lambda i: (i, 0)
    const = lambda i: (0, 0)
    return pl.pallas_call(
        _proj_ln_body,
        grid=(t // tm,),
        in_specs=[pl.BlockSpec((tm, k), row), pl.BlockSpec((k, d), const), pl.BlockSpec((tm, d), row),
                  pl.BlockSpec((1, d), const), pl.BlockSpec((1, d), const)],
        out_specs=pl.BlockSpec((tm, d), row),
        out_shape=jax.ShapeDtypeStruct((t, d), F32),
        compiler_params=_params("parallel"),
        name="proj_ln",
    )(y, w, h, g.reshape(1, d), b.reshape(1, d))


def _norm_then_ple(r, g_ref, b_ref, p_ref, pwg_ref, pbg_ref, pwp_ref):
    h = _layer_norm(r, g_ref[...], b_ref[...])
    gate = _sigmoid(_dot(h.astype(BF16), pwg_ref[...]) + pbg_ref[...])
    return h + gate * _dot(p_ref[...].astype(BF16), pwp_ref[...])


def _ffn_body(x_ref, wg_ref, wu_ref, wd_ref, g_ref, b_ref, p_ref, pwg_ref, pbg_ref, pwp_ref, o_ref, acc_ref):
    j = pl.program_id(1)

    @pl.when(j == 0)
    def _():
        acc_ref[...] = jnp.zeros_like(acc_ref)

    x = x_ref[...].astype(BF16)
    a = _dot(x, wg_ref[0])
    u = _dot(x, wu_ref[0])
    acc_ref[...] += _dot((_silu(a) * u).astype(BF16), wd_ref[0])

    @pl.when(j == pl.num_programs(1) - 1)
    def _():
        o_ref[...] = _norm_then_ple(ALPHA * x_ref[...] + acc_ref[...], g_ref, b_ref, p_ref.at[0], pwg_ref, pbg_ref,
                                    pwp_ref)


def _ffn(x, wg, wu, wd, layer, g, b, p, p_layer, ple_wg, ple_bg, ple_wp, *, tm=1024, tf=512):
    t, d = x.shape
    ff = wg.shape[2]
    pd = p.shape[2]
    tm = min(tm, t)
    const = lambda i, j: (0, 0)
    return pl.pallas_call(
        _ffn_body,
        grid=(t // tm, ff // tf),
        in_specs=[pl.BlockSpec((tm, d), lambda i, j: (i, 0)), pl.BlockSpec((1, d, tf), lambda i, j: (layer, 0, j)),
                  pl.BlockSpec((1, d, tf), lambda i, j: (layer, 0, j)), pl.BlockSpec((1, tf, d), lambda i, j: (layer, j, 0)),
                  pl.BlockSpec((1, d), const), pl.BlockSpec((1, d), const),
                  pl.BlockSpec((1, tm, pd), lambda i, j: (p_layer, i, 0)), pl.BlockSpec((d, d), const),
                  pl.BlockSpec((1, d), const), pl.BlockSpec((pd, d), const)],
        out_specs=pl.BlockSpec((tm, d), lambda i, j: (i, 0)),
        out_shape=jax.ShapeDtypeStruct((t, d), F32),
        scratch_shapes=[pltpu.VMEM((tm, d), F32)],
        compiler_params=_params("parallel", "arbitrary"),
        name="ffn",
    )(x, wg, wu, wd, g.reshape(1, d), b.reshape(1, d), p, ple_wg, ple_bg.reshape(1, d), ple_wp)


def _router_body(x_ref, w_ref, comb_ref, rank_ref, start_ref):
    logits = _dot_split(x_ref[...], w_ref[...])
    lane = lax.broadcasted_iota(jnp.int32, logits.shape, 1)
    neg = jnp.float32(-jnp.inf)
    logits = jnp.where(lane < N_EXPERTS, logits, neg)
    m1 = jnp.max(logits, axis=-1, keepdims=True)
    i1 = jnp.min(jnp.where(logits == m1, lane, LANES), axis=-1, keepdims=True)
    rest = jnp.where(lane == i1, neg, logits)
    m2 = jnp.max(rest, axis=-1, keepdims=True)
    i2 = jnp.min(jnp.where(rest == m2, lane, LANES), axis=-1, keepdims=True)
    e2 = jnp.exp(m2 - m1)
    w1 = 1.0 / (1.0 + e2)
    w2 = e2 / (1.0 + e2)
    comb_ref[...] = jnp.where(lane == i1, w1, 0.0) + jnp.where(lane == i2, w2, 0.0)
    chosen = jnp.where((lane == i1) | (lane == i2), 1.0, 0.0)
    tb = logits.shape[0]
    sub = MOE_SUB
    earlier = jnp.where(_tril(sub, strict=True), 1.0, 0.0).astype(BF16)
    before = jnp.zeros((1, LANES), F32)
    for s in range(tb // sub):
        rows = slice(s * sub, (s + 1) * sub)
        picked = chosen[rows]
        counts = _dot(earlier, picked.astype(BF16)) + before
        rank_ref[rows, :] = jnp.where(picked > 0.0, counts.astype(jnp.int32), -1)
        start_ref[0, s:s + 1, :] = before.astype(jnp.int32)
        before = before + jnp.sum(picked, axis=0, keepdims=True)
    start_ref[0, tb // sub:tb // sub + 1, :] = before.astype(jnp.int32)


def _router(x, w_router, *, tb):
    t, d = x.shape
    w = jnp.pad(w_router, ((0, 0), (0, LANES - w_router.shape[1])))
    row = lambda i: (i, 0)
    marks = tb // MOE_SUB + 1
    return pl.pallas_call(
        _router_body,
        grid=(t // tb,),
        in_specs=[pl.BlockSpec((tb, d), row), pl.BlockSpec((d, LANES), lambda i: (0, 0))],
        out_specs=[pl.BlockSpec((tb, LANES), row), pl.BlockSpec((tb, LANES), row),
                   pl.BlockSpec((1, marks, LANES), lambda i: (i, 0, 0))],
        out_shape=[jax.ShapeDtypeStruct((t, LANES), F32), jax.ShapeDtypeStruct((t, LANES), jnp.int32),
                   jax.ShapeDtypeStruct((t // tb, marks, LANES), jnp.int32)],
        compiler_params=_params("parallel"),
        name="router",
    )(x, w)


MOE_ROW_TILE = 128
MOE_MOVE_TILE = 256
MOE_SUB = 256


def _moe_body(start_ref, x_ref, comb_ref, rcol_ref, rrow_ref, wg_ref, wu_ref, wd_ref, g_ref, b_ref,
              p_ref, pwg_ref, pbg_ref, pwp_ref, o_ref, xb_s, xg_s, y_s):
    blk, e, j = pl.program_id(0), pl.program_id(1), pl.program_id(2)
    tm, tg, sub = MOE_ROW_TILE, MOE_MOVE_TILE, MOE_SUB
    tb, d = x_ref.shape
    n_sub = tb // sub
    base = (blk * pl.num_programs(1) + e) * (n_sub + 1)
    starts = [start_ref[base + s] for s in range(n_sub + 1)]
    n_tiles = (starts[n_sub] + (tm - 1)) // tm
    n_moves = (starts[n_sub] + (tg - 1)) // tg

    @pl.when((e == 0) & (j == 0))
    def _():
        xb_s[...] = x_ref[...].astype(BF16)
        o_ref[...] = jnp.zeros_like(o_ref)

    def tile_rows(t, size):
        return pl.ds(pl.multiple_of(t * size, size), size)

    def overlaps(s, lo):
        return (starts[s] < lo + tg) & (starts[s + 1] > lo)

    @pl.when(j == 0)
    def _():
        row = lax.broadcasted_iota(jnp.int32, (tg, sub), 0)

        def gather(t, carry):
            lo = t * tg
            rows = tile_rows(t, tg)
            y_s[rows, :] = jnp.zeros((tg, d), F32)
            for s in range(n_sub):
                @pl.when(overlaps(s, lo))
                def _():
                    slot_of_token = rrow_ref[0, pl.ds(e, 1), s * sub:(s + 1) * sub]
                    onehot = jnp.where(slot_of_token - lo == row, 1.0, 0.0).astype(BF16)
                    y_s[rows, :] += _dot(onehot, xb_s[s * sub:(s + 1) * sub, :])
            xg_s[rows, :] = y_s[rows, :].astype(BF16)
            y_s[rows, :] = jnp.zeros((tg, d), F32)
            return carry

        lax.fori_loop(0, n_moves, gather, 0)

    def expert(t, carry):
        rows = tile_rows(t, tm)
        xt = xg_s[rows, :]
        a = _dot(xt, wg_ref[0, 0])
        u = _dot(xt, wu_ref[0, 0])
        y_s[rows, :] += _dot((_silu(a) * u).astype(BF16), wd_ref[0, 0])
        return carry

    lax.fori_loop(0, n_tiles, expert, 0)

    @pl.when(j == pl.num_programs(2) - 1)
    def _():
        lane = lax.broadcasted_iota(jnp.int32, (sub, LANES), 1)
        col = lax.broadcasted_iota(jnp.int32, (sub, tg), 1)

        def scatter(t, carry):
            lo = t * tg
            yt = y_s[tile_rows(t, tg), :].astype(BF16)
            for s in range(n_sub):
                @pl.when(overlaps(s, lo))
                def _():
                    rs = slice(s * sub, (s + 1) * sub)
                    slot = jnp.sum(jnp.where(lane == e, rcol_ref[rs, :], 0), axis=-1, keepdims=True)
                    weight = jnp.sum(jnp.where(lane == e, comb_ref[rs, :], 0.0), axis=-1, keepdims=True)
                    onehot = jnp.where(slot - lo == col, 1.0, 0.0).astype(BF16)
                    o_ref[rs, :] += weight * _dot(onehot, yt)
            return carry

        lax.fori_loop(0, n_moves, scatter, 0)

    @pl.when((e == pl.num_programs(1) - 1) & (j == pl.num_programs(2) - 1))
    def _():
        def finish(r, carry):
            rs = pl.ds(pl.multiple_of(r * sub, sub), sub)
            o_ref[rs, :] = _norm_then_ple(ALPHA * x_ref[rs, :] + o_ref[rs, :], g_ref, b_ref, p_ref.at[0, rs, :],
                                          pwg_ref, pbg_ref, pwp_ref)
            return carry

        lax.fori_loop(0, n_sub, finish, 0)


def _moe(x, comb, rank, starts, wg, wu, wd, layer, g, b, p, p_layer, ple_wg, ple_bg, ple_wp, *, tb, tf=896):
    t, d = x.shape
    _, ne, _, ff = wg.shape
    pd = p.shape[2]
    nb = t // tb
    rank_rows = rank.reshape(nb, tb, LANES)[:, :, :ne].transpose(0, 2, 1)
    starts = starts[:, :, :ne].transpose(0, 2, 1).reshape(-1)
    once = pl.Buffered(1)
    grid_spec = pltpu.PrefetchScalarGridSpec(
        num_scalar_prefetch=1,
        grid=(nb, ne, ff // tf),
        in_specs=[pl.BlockSpec((tb, d), lambda i, e, j, c: (i, 0), pipeline_mode=once),
                  pl.BlockSpec((tb, LANES), lambda i, e, j, c: (i, 0), pipeline_mode=once),
                  pl.BlockSpec((tb, LANES), lambda i, e, j, c: (i, 0), pipeline_mode=once),
                  pl.BlockSpec((1, ne, tb), lambda i, e, j, c: (i, 0, 0), pipeline_mode=once),
                  pl.BlockSpec((1, 1, d, tf), lambda i, e, j, c: (layer, e, 0, j)),
                  pl.BlockSpec((1, 1, d, tf), lambda i, e, j, c: (layer, e, 0, j)),
                  pl.BlockSpec((1, 1, tf, d), lambda i, e, j, c: (layer, e, j, 0)),
                  pl.BlockSpec((1, d), lambda i, e, j, c: (0, 0)), pl.BlockSpec((1, d), lambda i, e, j, c: (0, 0)),
                  pl.BlockSpec((1, tb, pd), lambda i, e, j, c: (p_layer, i, 0), pipeline_mode=once),
                  pl.BlockSpec((d, d), lambda i, e, j, c: (0, 0), pipeline_mode=once),
                  pl.BlockSpec((1, d), lambda i, e, j, c: (0, 0)),
                  pl.BlockSpec((pd, d), lambda i, e, j, c: (0, 0), pipeline_mode=once)],
        out_specs=pl.BlockSpec((tb, d), lambda i, e, j, c: (i, 0), pipeline_mode=once),
        scratch_shapes=[pltpu.VMEM((tb, d), BF16), pltpu.VMEM((tb, d), BF16), pltpu.VMEM((tb, d), F32)],
    )
    return pl.pallas_call(
        _moe_body,
        grid_spec=grid_spec,
        out_shape=jax.ShapeDtypeStruct((t, d), F32),
        compiler_params=pltpu.CompilerParams(dimension_semantics=("parallel", "arbitrary", "arbitrary"),
                                             vmem_limit_bytes=MOE_VMEM_LIMIT_BYTES),
        name="moe",
    )(starts, x, comb, rank, rank_rows, wg, wu, wd, g.reshape(1, d), b.reshape(1, d), p, ple_wg, ple_bg.reshape(1, d),
      ple_wp)


def _moe_layer(h, w_router, wg, wu, wd, layer, g, b, p, p_layer, ple_wg, ple_bg, ple_wp, *, tb=2048):
    tb = min(tb, h.shape[0])
    comb, rank, starts = _router(h, w_router, tb=tb)
    return _moe(h, comb, rank, starts, wg, wu, wd, layer, g, b, p, p_layer, ple_wg, ple_bg, ple_wp, tb=tb)


def _gmlp_gate_body(u_ref, v_ref, lng_ref, lnb_ref, ws_ref, bs_ref, wo_ref, h_ref, g_ref, b_ref, o_ref, gated_ref):
    tm, width = u_ref.shape
    gw = width // GM_GROUPS
    v = _layer_norm(v_ref[...].astype(F32), lng_ref[...], lnb_ref[...]).astype(BF16)
    mask = _tril(GM_CHUNK)
    for g in range(GM_GROUPS):
        ws = jnp.where(mask, ws_ref[g], 0.0).astype(BF16)
        bias = bs_ref[:, g:g + 1]
        for c in range(tm // GM_CHUNK):
            rows = slice(c * GM_CHUNK, (c + 1) * GM_CHUNK)
            cols = slice(g * gw, (g + 1) * gw)
            sv = _dot(ws, v[rows, cols]) + bias
            gated_ref[rows, cols] = (u_ref[rows, cols].astype(F32) * sv).astype(BF16)
    acc = _dot(gated_ref[...], wo_ref[...])
    o_ref[...] = _layer_norm(ALPHA * h_ref[...] + acc, g_ref[...], b_ref[...])


def _gmlp_mixer(h, w_in, b_in, ln_g, ln_b, w_s, b_s, w_out, g, b, *, tm=512):
    t, d = h.shape
    width = w_out.shape[0]
    tm = min(tm, t)
    z = _linear_gelu(h, w_in.astype(BF16), b_in)
    row = lambda i: (i, 0)
    const = lambda i: (0, 0)
    return pl.pallas_call(
        _gmlp_gate_body,
        grid=(t // tm,),
        in_specs=[pl.BlockSpec((tm, width), row), pl.BlockSpec((tm, width), lambda i: (i, 1)),
                  pl.BlockSpec((1, width), const), pl.BlockSpec((1, width), const),
                  pl.BlockSpec((GM_GROUPS, GM_CHUNK, GM_CHUNK), lambda i: (0, 0, 0)),
                  pl.BlockSpec((GM_CHUNK, GM_GROUPS), const), pl.BlockSpec((width, d), const),
                  pl.BlockSpec((tm, d), row), pl.BlockSpec((1, d), const), pl.BlockSpec((1, d), const)],
        out_specs=pl.BlockSpec((tm, d), row),
        out_shape=jax.ShapeDtypeStruct((t, d), F32),
        scratch_shapes=[pltpu.VMEM((tm, width), BF16)],
        compiler_params=_params("parallel"),
        name="gmlp_gate",
    )(z, z, ln_g.reshape(1, width), ln_b.reshape(1, width), w_s, b_s.T, w_out.astype(BF16), h,
      g.reshape(1, d), b.reshape(1, d))


def _fox_decay_body(f_ref, bf_ref, c_ref):
    s = f_ref.shape[1]
    x = f_ref[0] + bf_ref[...]
    log_f = jnp.minimum(x, 0.0) - jnp.log1p(jnp.exp(-jnp.abs(x)))
    tri = _tril(LANES).astype(F32)
    carry = jnp.zeros((1, LANES), F32)
    lane = lax.broadcasted_iota(jnp.int32, (LANES, LANES), 1)
    hds = FOX_HEADS
    for i in range(s // LANES):
        cs = _dot_hi(tri, log_f[i * LANES:(i + 1) * LANES]) + carry
        carry = cs[LANES - 1:LANES]
        hi = cs.astype(BF16).astype(F32)
        mid = (cs - hi).astype(BF16).astype(F32)
        lo = (cs - hi - mid).astype(BF16).astype(F32)
        out = jnp.where(lane < hds, hi, 0.0)
        for slot, piece in enumerate((mid, lo, -hi, -mid, -lo), start=1):
            moved = pltpu.roll(piece, slot * hds, 1)
            out = jnp.where((lane >= slot * hds) & (lane < (slot + 1) * hds), moved, out)
        c_ref[0, i * LANES:(i + 1) * LANES, :] = out.astype(c_ref.dtype)


def _fox_decay(f, b_f):
    bsz, s, _ = f.shape
    blk = lambda i: (i, 0, 0)
    return pl.pallas_call(
        _fox_decay_body,
        grid=(bsz,),
        in_specs=[pl.BlockSpec((1, s, LANES), blk), pl.BlockSpec((1, LANES), lambda i: (0, 0))],
        out_specs=pl.BlockSpec((1, s, LANES), blk),
        out_shape=jax.ShapeDtypeStruct((bsz, s, LANES), BF16),
        compiler_params=_params("parallel"),
        name="fox_decay",
    )(f, b_f)


FOX_BIAS_LANES = 6


def _fox_attn_body(q_ref, k_ref, v_ref, cq_ref, ck_ref, o_ref, k_s, v_s):
    grp, qi = pl.program_id(1), pl.program_id(2)
    tq = q_ref.shape[1]
    half = FOX_HEAD_DIM
    pieces = FOX_BIAS_LANES // 2
    heads = range(FOX_HEADS_PER_STEP)
    lane = lax.broadcasted_iota(jnp.int32, (1, LANES), 1)
    src = lax.broadcasted_iota(jnp.int32, (LANES, LANES), 0)
    dst = lax.broadcasted_iota(jnp.int32, (LANES, LANES), 1)
    scale = jnp.where(lane < half, half ** -0.5, 1.0).astype(BF16)

    def head_features(ref, hh):
        pair = ref[0, :, (hh // 2) * LANES:(hh // 2 + 1) * LANES]
        sel = jnp.where((dst < half) & (src == dst + half * (hh % 2)), 1.0, 0.0).astype(BF16)
        return _dot(pair, sel)

    def bias_lanes(c6, hh, first_piece, at, ones_at):
        head = grp * FOX_HEADS_PER_STEP + hh
        piece = dst - at + first_piece
        sel = jnp.where((dst >= at) & (dst < at + pieces) & (src == piece * FOX_HEADS + head), 1.0, 0.0).astype(BF16)
        return _dot(c6, sel) + jnp.where((lane >= ones_at) & (lane < ones_at + pieces), 1.0, 0.0)

    @pl.when(qi == 0)
    def _():
        ck = ck_ref[0]
        for hh in heads:
            k_s[hh] = (head_features(k_ref, hh) + bias_lanes(ck, hh, pieces, half + pieces, half)).astype(BF16)
            v_aug = head_features(v_ref, hh) + jnp.where(lane >= half, 1.0, 0.0)
            for j in range(v_s.shape[1]):
                v_s[hh, j] = v_aug[j * tq:(j + 1) * tq].T.astype(BF16)

    cq = cq_ref[0]
    qs = [(head_features(q_ref, hh) + bias_lanes(cq, hh, 0, half, half + pieces)).astype(BF16) * scale for hh in heads]
    kv_row = lax.broadcasted_iota(jnp.int32, (tq, tq), 0)
    q_col = lax.broadcasted_iota(jnp.int32, (tq, tq), 1)
    diag = kv_row <= q_col

    def scores(kj):
        rows = pl.ds(pl.multiple_of(kj * tq, tq), tq)
        return tuple(_dot_nt(k_s[hh, rows, :], qs[hh]) for hh in heads)

    def accumulate(kj, ss, carry, masked):
        if masked:
            ss = [jnp.where(diag, s, -1e30) for s in ss]
        ms = [jnp.maximum(carry[hh][0], jnp.max(ss[hh], axis=0, keepdims=True)) for hh in heads]
        ps = [jnp.exp(ss[hh] - ms[hh]).astype(BF16) for hh in heads]
        pvs = [_dot(v_s[hh, kj], ps[hh]) for hh in heads]
        return tuple((ms[hh], jnp.exp(carry[hh][0] - ms[hh]) * carry[hh][1] + pvs[hh]) for hh in heads)

    def block(kj, state):
        ss, carry = state
        return scores(kj + 1), accumulate(kj, ss, carry, masked=False)

    init = tuple((jnp.full((1, tq), -1e30, F32), jnp.zeros((LANES, tq), F32)) for _ in heads)
    ss, carry = lax.fori_loop(0, qi, block, (scores(0), init))
    carry = accumulate(qi, ss, carry, masked=True)
    outs = [(acc / acc[half:half + 1, :]).T for _, acc in carry]
    for pr in range(len(outs) // 2):
        pair = jnp.where(lane < half, outs[2 * pr], pltpu.roll(outs[2 * pr + 1], half, 1))
        o_ref[0, :, pr * LANES:(pr + 1) * LANES] = pair.astype(o_ref.dtype)


FOX_HEADS_PER_STEP = 4


def _fox_attention(qkv, c6, *, tq=256):
    bsz, s, w3 = qkv.shape
    tq = min(tq, s)
    hps = FOX_HEADS_PER_STEP
    gw = hps * FOX_HEAD_DIM
    groups = w3 // (3 * gw)
    return pl.pallas_call(
        _fox_attn_body,
        grid=(bsz, groups, s // tq),
        in_specs=[pl.BlockSpec((1, tq, gw), lambda b, h, i: (b, i, h)),
                  pl.BlockSpec((1, s, gw), lambda b, h, i: (b, 0, groups + h)),
                  pl.BlockSpec((1, s, gw), lambda b, h, i: (b, 0, 2 * groups + h)),
                  pl.BlockSpec((1, tq, LANES), lambda b, h, i: (b, i, 0)),
                  pl.BlockSpec((1, s, LANES), lambda b, h, i: (b, 0, 0))],
        out_specs=pl.BlockSpec((1, tq, gw), lambda b, h, i: (b, i, h)),
        out_shape=jax.ShapeDtypeStruct((bsz, s, w3 // 3), BF16),
        scratch_shapes=[pltpu.VMEM((hps, s, LANES), BF16), pltpu.VMEM((hps, s // tq, LANES, tq), BF16)],
        compiler_params=_params("parallel", "parallel", "arbitrary"),
        name="fox_attn",
    )(qkv, qkv, qkv, c6, c6)


def _fox_mixer(h, bsz, w_in, b_f, w_out, g, b):
    t, d = h.shape
    s = t // bsz
    hds, dh = FOX_HEADS, FOX_HEAD_DIM
    width = hds * dh
    qkv = _linear(h, w_in[:, :3 * width].astype(BF16))
    w_f = jnp.pad(w_in[:, 3 * width:], ((0, 0), (0, LANES - hds)))
    f = _linear(h, w_f, hi=True, out_dtype=F32, tn=LANES)
    b_pad = jnp.pad(b_f, (0, LANES - hds)).reshape(1, LANES)
    c6 = _fox_decay(f.reshape(bsz, s, LANES), b_pad)
    o = _fox_attention(qkv.reshape(bsz, s, 3 * width), c6)
    return _proj_ln(o.reshape(t, width), w_out.astype(BF16), h, g, b)


def _causal_conv(x, taps):
    k = len(taps)
    row = lax.broadcasted_iota(jnp.int32, x.shape, 0)
    out = x * taps[k - 1]
    for sh in range(1, k):
        shifted = jnp.where(row >= sh, pltpu.roll(x, sh, 0), 0.0)
        out = out + shifted * taps[k - 1 - sh]
    return out


def _gdn_gates_body(x_ref, dtb_ref, alog_ref, o_ref):
    x = x_ref[...]
    lane = lax.broadcasted_iota(jnp.int32, x.shape, 1)
    beta = _sigmoid(x)
    gdec = -jnp.exp(alog_ref[...]) * _softplus(x + dtb_ref[...])
    gdec = jnp.where((lane >= GDN_HEADS) & (lane < 2 * GDN_HEADS), gdec, 0.0)
    tri = _tril(GDN_CHUNK).astype(F32)
    lane_c = lax.broadcasted_iota(jnp.int32, (GDN_CHUNK, LANES), 1)
    for c in range(x.shape[0] // GDN_CHUNK):
        rows = slice(c * GDN_CHUNK, (c + 1) * GDN_CHUNK)
        gc = _dot_hi(tri, gdec[rows])
        o_ref[rows, :] = jnp.where(lane_c < GDN_HEADS, beta[rows], gc)


def _gdn_gates(x, dt_bias, a_log, *, tm=512):
    t = x.shape[0]
    tm = min(tm, t)
    pad = lambda v: jnp.pad(v, (GDN_HEADS, LANES - 2 * GDN_HEADS)).reshape(1, LANES)
    return pl.pallas_call(
        _gdn_gates_body,
        grid=(t // tm,),
        in_specs=[pl.BlockSpec((tm, LANES), lambda i: (i, 0)), pl.BlockSpec((1, LANES), lambda i: (0, 0)),
                  pl.BlockSpec((1, LANES), lambda i: (0, 0))],
        out_specs=pl.BlockSpec((tm, LANES), lambda i: (i, 0)),
        out_shape=jax.ShapeDtypeStruct((t, LANES), F32),
        compiler_params=_params("parallel"),
        name="gdn_gates",
    )(x, pad(dt_bias), pad(a_log))


def _unit_lower_inverses(ms):
    n = ms[0].shape[0]
    eye = (lax.broadcasted_iota(jnp.int32, (n, n), 0) == lax.broadcasted_iota(jnp.int32, (n, n), 1)).astype(F32)
    ps = [-m for m in ms]
    invs = [eye + p for p in ps]
    step = 1
    while 2 * step < n:
        ps = [_dot_split(p, p) for p in ps]
        invs = [inv + _dot_split(inv, p) for inv, p in zip(invs, ps)]
        step *= 2
    return invs


GDN_HEADS_PER_STEP = 4
GDN_CHUNKS_PER_ITER = 2


def _gdn_body(q_ref, k_ref, v_ref, gate_ref, cw_ref, gcol_ref, grow_ref, ng_ref, o_ref,
              q_s, k_s, uw_s, g_s, a_s, o_s):
    hb = GDN_HEADS_PER_STEP
    grp = GDN_CHUNKS_PER_ITER
    hd0 = pl.program_id(1) * hb
    s = q_ref.shape[1]
    c = GDN_CHUNK
    dk = GDN_DK
    n_chunks = s // c

    gcols = gcol_ref[0]
    lane = lax.broadcasted_iota(jnp.int32, gcols.shape, 1)
    for hh in range(hb):
        cols = slice(hh * dk, (hh + 1) * dk)

        def conv_act(x_ref, which):
            taps = [cw_ref[hh, which, i:i + 1, :] for i in range(cw_ref.shape[2])]
            return _silu(_causal_conv(x_ref[0, :, cols].astype(F32), taps))

        q = conv_act(q_ref, 0)
        k = conv_act(k_ref, 1)
        v = conv_act(v_ref, 2)
        q = q * lax.rsqrt(jnp.sum(q * q, axis=-1, keepdims=True) + L2_EPS) * (dk ** -0.5)
        k = k * lax.rsqrt(jnp.sum(k * k, axis=-1, keepdims=True) + L2_EPS)
        beta = jnp.sum(jnp.where(lane == hd0 + hh, gcols, 0.0), axis=-1, keepdims=True)
        gc = jnp.sum(jnp.where(lane == hd0 + hh + GDN_HEADS, gcols, 0.0), axis=-1, keepdims=True)
        kb = k * beta
        q_s[hh] = q
        k_s[hh] = k
        uw_s[hh, :, 0:dk] = v * beta
        uw_s[hh, :, dk:2 * dk] = kb * jnp.exp(gc)
        g_s[:, hh:hh + 1] = gc
        g_s[:, hb + hh:hb + hh + 1] = beta
    causal = _tril(c)
    strict = _tril(c, strict=True)

    def prep(it, carry):
        base = pl.multiple_of(it * (grp * c), grp * c)
        where, ms, aqks = [], [], []
        for hh in range(hb):
            for gi in range(grp):
                rows = pl.ds(base + gi * c, c)
                kc = k_s[hh, rows, :]
                gcc = g_s[rows, hh:hh + 1]
                grow = grow_ref[0, hh, pl.ds(it * grp + gi, 1), :]
                decay = jnp.exp(jnp.where(causal, gcc - grow, -jnp.inf))
                kb = kc * g_s[rows, hb + hh:hb + hh + 1]
                ms.append(jnp.where(strict, _dot_nt(kb, kc) * decay, 0.0))
                aqks.append(_dot_nt(q_s[hh, rows, :], kc) * decay)
                where.append((hh, gi, rows))
        invs = _unit_lower_inverses(ms)
        uws = [_dot_split(inv, uw_s[hh, rows, :]) for inv, (hh, gi, rows) in zip(invs, where)]
        for (hh, gi, rows), aqk, uw in zip(where, aqks, uws):
            a_s[hh, it * grp + gi] = aqk
            uw_s[hh, rows, :] = uw
            gcc = g_s[rows, hh:hh + 1]
            g_last = g_s[pl.ds(base + gi * c + c - 1, 1), hh:hh + 1]
            q_s[hh, rows, :] = q_s[hh, rows, :] * jnp.exp(gcc)
            k_s[hh, rows, :] = k_s[hh, rows, :] * jnp.exp(g_last - gcc)
        return carry

    lax.fori_loop(0, n_chunks // grp, prep, 0)

    def scan(i, states):
        rows = pl.ds(pl.multiple_of(i * c, c), c)
        heads = range(hb)
        v_news = [uw_s[hh, rows, 0:dk] - _dot(uw_s[hh, rows, dk:2 * dk], states[hh]) for hh in heads]
        for hh in heads:
            o_s[hh, rows, :] = _dot(q_s[hh, rows, :], states[hh]) + _dot(a_s[hh, i], v_news[hh])
        decays = [jnp.exp(g_s[pl.ds(i * c + c - 1, 1), hh:hh + 1]) for hh in heads]
        return tuple(states[hh] * decays[hh] + _dot_tn(k_s[hh, rows, :], v_news[hh]) for hh in heads)

    lax.fori_loop(0, n_chunks, scan, tuple(jnp.zeros((dk, dk), F32) for _ in range(hb)))
    for hh in range(hb):
        cols = slice(hh * dk, (hh + 1) * dk)
        o = o_s[hh]
        o = o * lax.rsqrt(jnp.mean(o * o, axis=-1, keepdims=True) + RMS_EPS) * ng_ref[...]
        o_ref[0, :, cols] = (o * _silu(gate_ref[0, :, cols].astype(F32))).astype(o_ref.dtype)


def _gdn_core(proj, conv_w, gcol, grow, norm_g):
    bsz, s, _ = proj.shape
    hd, dk, c, hb = GDN_HEADS, GDN_DK, GDN_CHUNK, GDN_HEADS_PER_STEP
    n = s // c
    steps = hd // hb
    cw = conv_w.reshape(conv_w.shape[0], 3, hd, dk).transpose(2, 1, 0, 3)
    kk = conv_w.shape[0]
    seq = lambda off: pl.BlockSpec((1, s, hb * dk), lambda b, h: (b, 0, off + h))
    per_head = lambda *shape: pltpu.VMEM((hb,) + shape, F32)
    return pl.pallas_call(
        _gdn_body,
        grid=(bsz, steps),
        in_specs=[seq(0), seq(steps), seq(2 * steps), seq(3 * steps),
                  pl.BlockSpec((hb, 3, kk, dk), lambda b, h: (h, 0, 0, 0)),
                  pl.BlockSpec((1, s, LANES), lambda b, h: (b, 0, 0)),
                  pl.BlockSpec((1, hb, n, c), lambda b, h: (b, h, 0, 0)),
                  pl.BlockSpec((1, dk), lambda b, h: (0, 0))],
        out_specs=seq(0),
        out_shape=jax.ShapeDtypeStruct((bsz, s, hd * dk), BF16),
        scratch_shapes=[per_head(s, dk), per_head(s, dk), per_head(s, 2 * dk), pltpu.VMEM((s, LANES), F32),
                        per_head(n, c, c), per_head(s, dk)],
        compiler_params=_params("parallel", "arbitrary"),
        name="gdn_core",
    )(proj, proj, proj, proj, cw, gcol, grow, norm_g.reshape(1, dk))


def _gdn_mixer(h, bsz, w_in, conv_w, a_log, dt_bias, norm_g, w_out, g, b):
    t, d = h.shape
    s = t // bsz
    hd = GDN_HEADS
    qkv_w = 3 * hd * GDN_DK
    gate_w = hd * GDN_DK
    proj = _linear(h, w_in[:, :qkv_w + gate_w].astype(BF16))
    w_small = jnp.pad(w_in[:, qkv_w + gate_w:], ((0, 0), (0, LANES - 2 * hd)))
    logits = _linear(h, w_small, hi=True, out_dtype=F32, tn=LANES)
    gcol = _gdn_gates(logits, dt_bias, a_log).reshape(bsz, s, LANES)
    grow = gcol[:, :, hd:2 * hd].transpose(0, 2, 1).reshape(bsz, hd, s // GDN_CHUNK, GDN_CHUNK)
    o = _gdn_core(proj.reshape(bsz, s, qkv_w + gate_w), conv_w, gcol, grow, norm_g)
    return _proj_ln(o.reshape(t, gate_w), w_out.astype(BF16), h, g, b)


def _ssd_gates_body(x_ref, dtb_ref, alog_ref, o_ref):
    x = x_ref[...]
    lane = lax.broadcasted_iota(jnp.int32, x.shape, 1)
    dt = _softplus(x + dtb_ref[...])
    a = jnp.where(lane < SSD_HEADS, dt * (-jnp.exp(alog_ref[...])), 0.0)
    tri = _tril(SSD_CHUNK).astype(F32)
    lane_c = lax.broadcasted_iota(jnp.int32, (SSD_CHUNK, LANES), 1)
    for c in range(x.shape[0] // SSD_CHUNK):
        rows = slice(c * SSD_CHUNK, (c + 1) * SSD_CHUNK)
        acum = pltpu.roll(_dot_hi(tri, a[rows]), SSD_HEADS, 1)
        o_ref[rows, :] = jnp.where(lane_c < SSD_HEADS, dt[rows], acum)


def _ssd_gates(x, dt_bias, a_log, *, tm=512):
    t = x.shape[0]
    tm = min(tm, t)
    pad = lambda v: jnp.pad(v, (0, LANES - SSD_HEADS)).reshape(1, LANES)
    return pl.pallas_call(
        _ssd_gates_body,
        grid=(t // tm,),
        in_specs=[pl.BlockSpec((tm, LANES), lambda i: (i, 0)), pl.BlockSpec((1, LANES), lambda i: (0, 0)),
                  pl.BlockSpec((1, LANES), lambda i: (0, 0))],
        out_specs=pl.BlockSpec((tm, LANES), lambda i: (i, 0)),
        out_shape=jax.ShapeDtypeStruct((t, LANES), F32),
        compiler_params=_params("parallel"),
        name="ssd_gates",
    )(x, pad(dt_bias), pad(a_log))


def _ssd_body(x_ref, b_ref, c_ref, z_ref, cwx_ref, cwb_ref, cwc_ref, cbx_ref, cbb_ref, cbc_ref,
              gcol_ref, arow_ref, d_ref, ng_ref, o_ref, x_s, b_s, c_s, y_s):
    grp = pl.program_id(1)
    s = x_ref.shape[1]
    c = SSD_CHUNK
    hp = SSD_HEAD_DIM
    heads = x_ref.shape[2] // hp
    n_chunks = s // c

    def conv_act(ref, w_ref, bias_ref):
        taps = [w_ref[i:i + 1, :] for i in range(w_ref.shape[0])]
        return _silu(_causal_conv(ref[0].astype(F32), taps) + bias_ref[...])

    x_s[...] = conv_act(x_ref, cwx_ref, cbx_ref)
    b_s[...] = conv_act(b_ref, cwb_ref, cbb_ref)
    c_s[...] = conv_act(c_ref, cwc_ref, cbc_ref)
    causal = _tril(c)
    pair_lane = lax.broadcasted_iota(jnp.int32, (c, 2 * hp), 1) < hp

    def head_cols(gcol, base):
        lane = lax.broadcasted_iota(jnp.int32, gcol.shape, 1)
        return [jnp.sum(jnp.where(lane == base + grp * heads + j, gcol, 0.0), axis=-1, keepdims=True)
                for j in range(heads)]

    def widen(cols):
        pairs = [jnp.where(pair_lane, cols[2 * p], cols[2 * p + 1]) for p in range(heads // 2)]
        return jnp.concatenate(pairs, axis=1)

    def step(i, state):
        rows = pl.ds(pl.multiple_of(i * c, c), c)
        gcol = gcol_ref[0, rows, :]
        dt = head_cols(gcol, 0)
        acum = head_cols(gcol, SSD_HEADS)
        arow = arow_ref[0, 0, i]
        xc = x_s[rows, :]
        bc = b_s[rows, :]
        cc = c_s[rows, :]
        xdt = xc * widen(dt)
        cb = _dot_nt(cc, bc)
        y = _dot(cc, state) * widen([jnp.exp(a) for a in acum])
        intra = []
        for p in range(heads // 2):
            xp = xdt[:, 2 * p * hp:(2 * p + 2) * hp]
            ys = []
            for j in (2 * p, 2 * p + 1):
                lmat = jnp.exp(jnp.where(causal, acum[j] - arow[j:j + 1, :], -jnp.inf))
                ys.append(_dot(cb * lmat, xp))
            intra.append(jnp.where(pair_lane, ys[0], ys[1]))
        y = y + jnp.concatenate(intra, axis=1) + xc * d_ref[...]
        y_s[rows, :] = y
        last = head_cols(gcol_ref[0, pl.ds(i * c + c - 1, 1), :], SSD_HEADS)
        xdec = xdt * widen([jnp.exp(last[j] - acum[j]) for j in range(heads)])
        first_lane = lax.broadcasted_iota(jnp.int32, (1, 2 * hp), 1) < hp
        lastw = jnp.concatenate([jnp.where(first_lane, jnp.exp(last[2 * p]), jnp.exp(last[2 * p + 1]))
                                 for p in range(heads // 2)], axis=1)
        return state * lastw + _dot_tn(bc, xdec)

    lax.fori_loop(0, n_chunks, step, jnp.zeros((SSD_STATE, x_ref.shape[2]), F32))
    y = y_s[...] * _silu(z_ref[0].astype(F32))
    y = y * lax.rsqrt(jnp.mean(y * y, axis=-1, keepdims=True) + RMS_EPS) * ng_ref[...]
    o_ref[0] = y.astype(o_ref.dtype)


def _ssd_core(proj, conv_w, conv_b, gcol, arow, d_wide, norm_g):
    bsz, s, _ = proj.shape
    grp, n_state, c = SSD_GROUPS, SSD_STATE, SSD_CHUNK
    inner = SSD_HEADS * SSD_HEAD_DIM
    gw = inner // grp
    heads = SSD_HEADS // grp
    n = s // c
    kk = conv_w.shape[0]
    xblk = lambda off: pl.BlockSpec((1, s, gw), lambda b, g: (b, 0, off + g))
    nblk = lambda off: pl.BlockSpec((1, s, n_state), lambda b, g: (b, 0, off + g))
    wx = lambda rows, off: pl.BlockSpec((rows, gw), lambda b, g: (0, off + g))
    wn = lambda rows, off: pl.BlockSpec((rows, n_state), lambda b, g: (0, off + g))
    x_off = inner // gw
    b_off = 2 * inner // n_state
    c_off = b_off + grp
    return pl.pallas_call(
        _ssd_body,
        grid=(bsz, grp),
        in_specs=[xblk(x_off), nblk(b_off), nblk(c_off), xblk(0),
                  wx(kk, 0), wn(kk, inner // n_state), wn(kk, inner // n_state + grp),
                  wx(1, 0), wn(1, inner // n_state), wn(1, inner // n_state + grp),
                  pl.BlockSpec((1, s, LANES), lambda b, g: (b, 0, 0)),
                  pl.BlockSpec((1, 1, n, heads, c), lambda b, g: (b, g, 0, 0, 0)),
                  wx(1, 0), wx(1, 0)],
        out_specs=xblk(0),
        out_shape=jax.ShapeDtypeStruct((bsz, s, inner), BF16),
        scratch_shapes=[pltpu.VMEM((s, gw), F32), pltpu.VMEM((s, n_state), F32), pltpu.VMEM((s, n_state), F32),
                        pltpu.VMEM((s, gw), F32)],
        compiler_params=_params("parallel", "arbitrary"),
        name="ssd_core",
    )(proj, proj, proj, proj, conv_w, conv_w, conv_w, conv_b, conv_b, conv_b, gcol, arow, d_wide, norm_g)


def _ssd_mixer(h, bsz, w_in, conv_w, conv_b, dt_bias, a_log, d_skip, norm_g, w_out, g, b):
    t, d = h.shape
    s = t // bsz
    inner = SSD_HEADS * SSD_HEAD_DIM
    main = 2 * inner + 2 * SSD_GROUPS * SSD_STATE
    heads = SSD_HEADS // SSD_GROUPS
    proj = _linear(h, w_in[:, :main].astype(BF16))
    w_dt = jnp.pad(w_in[:, main:], ((0, 0), (0, LANES - SSD_HEADS)))
    dt_logits = _linear(h, w_dt, hi=True, out_dtype=F32, tn=LANES)
    gcol = _ssd_gates(dt_logits, dt_bias, a_log).reshape(bsz, s, LANES)
    arow = gcol[:, :, SSD_HEADS:2 * SSD_HEADS].reshape(bsz, s // SSD_CHUNK, SSD_CHUNK, SSD_GROUPS, heads)
    arow = arow.transpose(0, 3, 1, 4, 2)
    d_wide = jnp.repeat(d_skip, SSD_HEAD_DIM).reshape(1, inner)
    o = _ssd_core(proj.reshape(bsz, s, main), conv_w, conv_b.reshape(1, -1), gcol, arow, d_wide,
                  norm_g.reshape(1, inner))
    return _proj_ln(o.reshape(t, inner), w_out.astype(BF16), h, g, b)


def kernel(x, p, ln_g, ln_b, gm_w_in, gm_b_in, gm_ln_g, gm_ln_b, gm_w_s, gm_b_s, gm_w_out, fox_w_in, fox_b_f, fox_w_out, gdn_w_in, gdn_conv_w, gdn_a_log, gdn_dt_bias, gdn_norm_g, gdn_w_out, ssd_w_in, ssd_conv_w, ssd_conv_b, ssd_dt_bias, ssd_a_log, ssd_d, ssd_norm_g, ssd_w_out, ffn_w_gate, ffn_w_up, ffn_w_down, moe_w_router, moe_w_gate, moe_w_up, moe_w_down, ple_w_gate, ple_b_gate, ple_w_proj):
    bsz, s, d = x.shape
    t = bsz * s
    h = x.reshape(t, d)
    pf = p.reshape(p.shape[0], t, p.shape[-1])
    ffn_w = tuple(w.astype(BF16) for w in (ffn_w_gate, ffn_w_up, ffn_w_down))
    moe_w = tuple(w.astype(BF16) for w in (moe_w_gate, moe_w_up, moe_w_down))
    for i in range(DEPTH):
        m, j = i % 4, i // 4
        g0, b0 = ln_g[i, 0], ln_b[i, 0]
        if m == 0:
            h = _gmlp_mixer(h, gm_w_in[j], gm_b_in[j], gm_ln_g[j], gm_ln_b[j], gm_w_s[j], gm_b_s[j], gm_w_out[j], g0, b0)
        elif m == 1:
            h = _fox_mixer(h, bsz, fox_w_in[j], fox_b_f[j], fox_w_out[j], g0, b0)
        elif m == 2:
            h = _gdn_mixer(h, bsz, gdn_w_in[j], gdn_conv_w[j], gdn_a_log[j], gdn_dt_bias[j], gdn_norm_g[j], gdn_w_out[j],
                           g0, b0)
        else:
            h = _ssd_mixer(h, bsz, ssd_w_in[j], ssd_conv_w[j], ssd_conv_b[j], ssd_dt_bias[j], ssd_a_log[j], ssd_d[j],
                           ssd_norm_g[j], ssd_w_out[j], g0, b0)
        g1, b1 = ln_g[i, 1], ln_b[i, 1]
        ple = (pf, i, ple_w_gate[i].astype(BF16), ple_b_gate[i], ple_w_proj[i].astype(BF16))
        if i % 2 == 0:
            h = _ffn(h, *ffn_w, i // 2, g1, b1, *ple)
        else:
            h = _moe_layer(h, moe_w_router[i // 2], *moe_w, i // 2, g1, b1, *ple)
    return h.reshape(bsz, s, d)
```
